```python
import math
import jax, jax.numpy as jnp
from jax import lax
import numpy as np

D_MODEL = 1024
BATCH = 4
SEQ = 4096
DEPTH = 1

MEM_LEN = 256
EPS = 1e-6
CONV_WIDTH = CONV_HEADS = None
CONV_K = 3
CONV_GROUPS = 8
CONV_DIM = D_MODEL
GM_HEADS = 8
GM_HEAD_DIM = D_MODEL // GM_HEADS
GM_DIM = GM_HEADS * GM_HEAD_DIM
CHUNK = 128
MIX_DIM = CONV_DIM + GM_DIM
IN_DIM = 4 * CONV_DIM + 3 * GM_DIM
X_HEADS = 4
X_HEAD_DIM = D_MODEL // X_HEADS

kernel_name = "hybrid_shortconv_gmlp_memxattn_block"


def rms_norm(x, g):
    xf = x.astype(jnp.float32)
    y = xf * lax.rsqrt(jnp.mean(xf * xf, axis=-1, keepdims=True) + EPS)
    return (y * g.astype(jnp.float32)).astype(x.dtype)


def causal_depthwise_conv(h, w):
    c = h.shape[-1]
    return lax.conv_general_dilated(
        h, w[:, None, :].astype(h.dtype), window_strides=(1,),
        padding=[(CONV_K - 1, 0)], dimension_numbers=("NWC", "WIO", "NWC"),
        feature_group_count=c)


def chunked_spatial_gating(u, v, ln_g, ln_b, ws, bs):
    b, s, _ = v.shape
    n = s // CHUNK
    vh = v.reshape(b, n, CHUNK, GM_HEADS, GM_HEAD_DIM).astype(jnp.float32)
    mu = jnp.mean(vh, axis=-1, keepdims=True)
    var = jnp.mean(jnp.square(vh - mu), axis=-1, keepdims=True)
    vn = (vh - mu) * lax.rsqrt(var + EPS)
    vn = (vn * ln_g.reshape(GM_HEADS, GM_HEAD_DIM).astype(jnp.float32)
          + ln_b.reshape(GM_HEADS, GM_HEAD_DIM).astype(jnp.float32)).astype(v.dtype)
    mask = jnp.tril(jnp.ones((CHUNK, CHUNK), dtype=bool))
    w_c = jnp.where(mask[None], ws, jnp.zeros_like(ws))
    sp = jnp.einsum("hts,bnshc->bnthc", w_c, vn) + bs.T[:, :, None]
    return u * sp.reshape(b, s, GM_DIM)


def mixer_sublayer(h, w_in, conv_w, gm_ln_g, gm_ln_b, gm_ws, gm_bs, w_out):
    proj = h @ w_in
    gb, gc, xa, za, u, v, zb = jnp.split(
        proj, np.cumsum([CONV_DIM] * 4 + [GM_DIM] * 2).tolist(), axis=-1)
    a = gb * causal_depthwise_conv(gc * xa, conv_w)
    a = a * jax.nn.silu(za)
    bo = chunked_spatial_gating(jax.nn.gelu(u), jax.nn.gelu(v),
                                gm_ln_g, gm_ln_b, gm_ws, gm_bs)
    bo = bo * jax.nn.silu(zb)
    return jnp.concatenate([a, bo], axis=-1) @ w_out


def memory_cross_attention(h, m, w_q, w_kv, w_xo):
    b, s, _ = h.shape
    q = (h @ w_q).reshape(b, s, X_HEADS, X_HEAD_DIM)
    k, vv = jnp.split(m @ w_kv, 2, axis=-1)
    k = k.reshape(b, MEM_LEN, X_HEADS, X_HEAD_DIM)
    vv = vv.reshape(b, MEM_LEN, X_HEADS, X_HEAD_DIM)
    scores = jnp.einsum("bshd,bmhd->bhsm", q, k).astype(jnp.float32)
    p = jax.nn.softmax(scores * (1.0 / math.sqrt(X_HEAD_DIM)), axis=-1).astype(vv.dtype)
    o = jnp.einsum("bhsm,bmhd->bshd", p, vv).reshape(b, s, D_MODEL)
    return o @ w_xo


def setup_inputs(seed: int = 0) -> dict:
    key = jax.random.key(seed)
    ks = jax.random.split(key, 20)
    f32 = jnp.float32
    L = DEPTH
    nrm = lambda k, shape, scale: jax.random.normal(k, shape, f32) * scale
    return {
        "x": nrm(ks[0], (BATCH, SEQ, D_MODEL), 1.0),
        "mem": nrm(ks[1], (BATCH, MEM_LEN, D_MODEL), 1.0),
        "norm_mix_g": 1.0 + nrm(ks[2], (L, D_MODEL), 0.02),
        "w_in": nrm(ks[3], (L, D_MODEL, IN_DIM), D_MODEL ** -0.5),
        "conv_w": nrm(ks[4], (L, CONV_K, CONV_DIM), CONV_K ** -0.5),
        "gm_ln_g": 1.0 + nrm(ks[5], (L, GM_DIM), 0.02),
        "gm_ln_b": nrm(ks[6], (L, GM_DIM), 0.02),
        "gm_ws": nrm(ks[7], (L, GM_HEADS, CHUNK, CHUNK), 0.5 * CHUNK ** -0.5),
        "gm_bs": 1.0 + nrm(ks[8], (L, GM_HEADS, CHUNK), 0.02),
        "w_out": nrm(ks[9], (L, MIX_DIM, D_MODEL), MIX_DIM ** -0.5),
        "norm_x_g": 1.0 + nrm(ks[10], (L, D_MODEL), 0.02),
        "norm_mem_g": 1.0 + nrm(ks[11], (L, D_MODEL), 0.02),
        "w_q": nrm(ks[12], (L, D_MODEL, D_MODEL), D_MODEL ** -0.5),
        "w_kv": nrm(ks[13], (L, D_MODEL, 2 * D_MODEL), D_MODEL ** -0.5),
        "w_xo": nrm(ks[14], (L, D_MODEL, D_MODEL), D_MODEL ** -0.5),
        "norm_final_g": 1.0 + nrm(ks[15], (D_MODEL,), 0.02),
    }


def reference(x, mem, norm_mix_g, w_in, conv_w, gm_ln_g, gm_ln_b, gm_ws, gm_bs,
              w_out, norm_x_g, norm_mem_g, w_q, w_kv, w_xo, norm_final_g):
    for l in range(DEPTH):
        h = rms_norm(x, norm_mix_g[l])
        x = x + mixer_sublayer(h, w_in[l], conv_w[l], gm_ln_g[l], gm_ln_b[l],
                               gm_ws[l], gm_bs[l], w_out[l])
        h = rms_norm(x, norm_x_g[l])
        m = rms_norm(mem, norm_mem_g[l])
        x = x + memory_cross_attention(h, m, w_q[l], w_kv[l], w_xo[l])
    return rms_norm(x, norm_final_g)
```

```python
import functools
import math

import jax
import jax.numpy as jnp
from jax import lax
from jax.experimental import pallas as pl
from jax.experimental.pallas import tpu as pltpu

EPS = 1e-6
CONV_K = 3
CHUNK = 128
HEAD_DIM = 128
N_SECT = 7
X_HEADS = 4
SUBLANES = 8

TM = 512
CW = 256
VMEM_LIMIT = 56 * 1024 * 1024


def _rms(x, g):
    ms = jnp.mean(x * x, axis=-1, keepdims=True)
    return x * lax.rsqrt(ms + EPS) * g


def _sigmoid(z):
    return 0.5 * (1.0 + jnp.tanh(0.5 * z))


def _gelu_tanh(x):
    c = math.sqrt(2.0 / math.pi)
    return 0.5 * x * (1.0 + jnp.tanh(c * (x + 0.044715 * (x * x * x))))


def _kv_kernel(mem_ref, g_ref, w_kv_ref, kt_ref, v_ref):
    d = mem_ref.shape[-1]
    m = _rms(mem_ref[...], g_ref[...]).astype(jnp.bfloat16)
    kv = jnp.dot(m, w_kv_ref[...], preferred_element_type=jnp.float32)
    kt_ref[...] = kv[:, :d].T.astype(jnp.bfloat16)
    v_ref[...] = kv[:, d:].astype(jnp.bfloat16)


def _mixer_kernel(x_ref, g_ref, w_in_ref, conv_w_ref, ln_g_ref, ln_b_ref,
                  ws_ref, bst_ref, w_out_ref, o_ref,
                  h_ref, proj_ref, carry_ref, mix_ref):
    tm, d = x_ref.shape
    n_col_chunks = d // CW
    n_row_blocks = tm // CHUNK
    heads_per_chunk = CW // HEAD_DIM

    @pl.when(pl.program_id(1) == 0)
    def _():
        carry_ref[...] = jnp.zeros_like(carry_ref)

    h_ref[...] = _rms(x_ref[...], g_ref[...]).astype(jnp.bfloat16)

    tri = (lax.broadcasted_iota(jnp.int32, (CHUNK, CHUNK), 0)
           >= lax.broadcasted_iota(jnp.int32, (CHUNK, CHUNK), 1))

    for j in range(n_col_chunks):
        cols = slice(j * CW, (j + 1) * CW)
        proj_ref[...] = jnp.dot(
            h_ref[...], w_in_ref[:, j * N_SECT * CW:(j + 1) * N_SECT * CW],
            preferred_element_type=jnp.float32)

        def sect(s, rows):
            return proj_ref[rows, s * CW:(s + 1) * CW]

        w0 = conv_w_ref[0:1, cols]
        w1 = conv_w_ref[1:2, cols]
        w2 = conv_w_ref[2:3, cols]
        wc = [jnp.where(tri, ws_ref[j * heads_per_chunk + hh], 0.0).astype(jnp.bfloat16)
              for hh in range(heads_per_chunk)]

        for r in range(n_row_blocks):
            rows = slice(r * CHUNK, (r + 1) * CHUNK)
            g = sect(1, rows) * sect(2, rows)
            if r == 0:
                prev = carry_ref[:, cols]
            else:
                prows = slice(r * CHUNK - SUBLANES, r * CHUNK)
                prev = sect(1, prows) * sect(2, prows)
            if r == n_row_blocks - 1:
                carry_ref[:, cols] = g[CHUNK - SUBLANES:, :]
            ext = jnp.concatenate([prev, g], axis=0)
            g1 = ext[SUBLANES - 1:SUBLANES - 1 + CHUNK, :]
            g2 = ext[SUBLANES - 2:SUBLANES - 2 + CHUNK, :]
            conv = w0 * g2 + w1 * g1 + w2 * g
            za = sect(3, rows)
            a = sect(0, rows) * conv * (za * _sigmoid(za))
            mix_ref[rows, cols] = a.astype(jnp.bfloat16)

            u = _gelu_tanh(sect(4, rows))
            v = _gelu_tanh(sect(5, rows))
            zb = sect(6, rows)
            sps = []
            for hh in range(heads_per_chunk):
                hc = slice(hh * HEAD_DIM, (hh + 1) * HEAD_DIM)
                gcols = slice(j * CW + hh * HEAD_DIM, j * CW + (hh + 1) * HEAD_DIM)
                vh = v[:, hc]
                mu = jnp.mean(vh, axis=-1, keepdims=True)
                vc = vh - mu
                var = jnp.mean(vc * vc, axis=-1, keepdims=True)
                vn = vc * lax.rsqrt(var + EPS) * ln_g_ref[:, gcols] + ln_b_ref[:, gcols]
                sp = jnp.dot(wc[hh], vn.astype(jnp.bfloat16),
                             preferred_element_type=jnp.float32)
                head = j * heads_per_chunk + hh
                sps.append(sp + bst_ref[:, head:head + 1])
            sp = jnp.concatenate(sps, axis=-1)
            bo = u * sp * (zb * _sigmoid(zb))
            mix_ref[rows, d + j * CW:d + (j + 1) * CW] = bo.astype(jnp.bfloat16)

    o_ref[...] = x_ref[...] + jnp.dot(mix_ref[...], w_out_ref[...],
                                      preferred_element_type=jnp.float32)


def _attn_kernel(x_ref, gx_ref, w_q_ref, kt_ref, v_ref, w_xo_ref, gf_ref, o_ref,
                 att_ref):
    tm, d = x_ref.shape
    hd = d // X_HEADS
    x = x_ref[...]
    h = _rms(x, gx_ref[...]).astype(jnp.bfloat16)
    q = jnp.dot(h, w_q_ref[...], preferred_element_type=jnp.float32)
    q = (q * (1.0 / math.sqrt(hd))).astype(jnp.bfloat16)
    for hh in range(X_HEADS):
        hc = slice(hh * hd, (hh + 1) * hd)
        s = jnp.dot(q[:, hc], kt_ref[hc, :], preferred_element_type=jnp.float32)
        p = jnp.exp(s - jnp.max(s, axis=-1, keepdims=True))
        l = jnp.sum(p, axis=-1, keepdims=True)
        o = jnp.dot(p.astype(jnp.bfloat16), v_ref[:, hc],
                    preferred_element_type=jnp.float32)
        att_ref[:, hc] = (o / l).astype(jnp.bfloat16)
    y = x + jnp.dot(att_ref[...], w_xo_ref[...], preferred_element_type=jnp.float32)
    o_ref[...] = _rms(y, gf_ref[...])


def _resident(shape):
    return pl.BlockSpec(shape, lambda *_: (0,) * len(shape),
                        pipeline_mode=pl.Buffered(1))


def kernel(x, mem, norm_mix_g, w_in, conv_w, gm_ln_g, gm_ln_b, gm_ws, gm_bs, w_out,
           norm_x_g, norm_mem_g, w_q, w_kv, w_xo, norm_final_g):
    b, s, d = x.shape
    m_len = mem.shape[1]
    assert w_in.shape[0] == 1, "the final norm is fused into the (single) layer's attention call"
    assert s % TM == 0 and TM % CHUNK == 0 and d % CW == 0
    bf16 = jnp.bfloat16
    params = pltpu.CompilerParams(
        dimension_semantics=("arbitrary", "arbitrary"), vmem_limit_bytes=VMEM_LIMIT)

    for l in range(1):
        w_in_c = (w_in[l].reshape(d, N_SECT, d // CW, CW).transpose(0, 2, 1, 3)
                  .reshape(d, N_SECT * d).astype(bf16))
        row = lambda a: a.reshape(1, -1)

        kt, v = pl.pallas_call(
            _kv_kernel,
            grid=(b,),
            in_specs=[pl.BlockSpec((None, m_len, d), lambda i: (i, 0, 0)),
                      _resident((1, d)), _resident((d, 2 * d))],
            out_specs=[pl.BlockSpec((None, d, m_len), lambda i: (i, 0, 0)),
                       pl.BlockSpec((None, m_len, d), lambda i: (i, 0, 0))],
            out_shape=[jax.ShapeDtypeStruct((b, d, m_len), bf16),
                       jax.ShapeDtypeStruct((b, m_len, d), bf16)],
            compiler_params=pltpu.CompilerParams(
                dimension_semantics=("arbitrary",), vmem_limit_bytes=VMEM_LIMIT),
            name="kv_proj",
        )(mem, row(norm_mem_g[l]), w_kv[l].astype(bf16))

        x = pl.pallas_call(
            _mixer_kernel,
            grid=(b, s // TM),
            in_specs=[pl.BlockSpec((None, TM, d), lambda i, t: (i, t, 0)),
                      _resident((1, d)),
                      _resident((d, N_SECT * d)),
                      _resident((CONV_K, d)),
                      _resident((1, d)), _resident((1, d)),
                      _resident(gm_ws.shape[1:]),
                      _resident((CHUNK, gm_bs.shape[1])),
                      _resident((2 * d, d))],
            out_specs=pl.BlockSpec((None, TM, d), lambda i, t: (i, t, 0)),
            out_shape=jax.ShapeDtypeStruct((b, s, d), jnp.float32),
            scratch_shapes=[pltpu.VMEM((TM, d), bf16),
                            pltpu.VMEM((TM, N_SECT * CW), jnp.float32),
                            pltpu.VMEM((SUBLANES, d), jnp.float32),
                            pltpu.VMEM((TM, 2 * d), bf16)],
            compiler_params=params,
            name="mixer",
        )(x, row(norm_mix_g[l]), w_in_c, conv_w[l], row(gm_ln_g[l]), row(gm_ln_b[l]),
          gm_ws[l], gm_bs[l].T, w_out[l].astype(bf16))

        x = pl.pallas_call(
            _attn_kernel,
            grid=(b, s // TM),
            in_specs=[pl.BlockSpec((None, TM, d), lambda i, t: (i, t, 0)),
                      _resident((1, d)),
                      _resident((d, d)),
                      pl.BlockSpec((None, d, m_len), lambda i, t: (i, 0, 0)),
                      pl.BlockSpec((None, m_len, d), lambda i, t: (i, 0, 0)),
                      _resident((d, d)),
                      _resident((1, d))],
            out_specs=pl.BlockSpec((None, TM, d), lambda i, t: (i, t, 0)),
            out_shape=jax.ShapeDtypeStruct((b, s, d), jnp.float32),
            scratch_shapes=[pltpu.VMEM((TM, d), bf16)],
            compiler_params=params,
            name="xattn",
        )(x, row(norm_x_g[l]), w_q[l].astype(bf16), kt, v, w_xo[l].astype(bf16),
          row(norm_final_g))
    return x
```

```python
import functools
import math

import jax
import jax.numpy as jnp
from jax import lax
from jax.experimental import pallas as pl
from jax.experimental.pallas import tpu as pltpu

EPS = 1e-6
CONV_K = 3
CHUNK = 128
HEAD_DIM = 128
N_SECT = 7
X_HEADS = 4
SUBLANES = 8

TM = 512
CW = 256
VMEM_LIMIT = 56 * 1024 * 1024


def _rms(x, g):
    ms = jnp.mean(x * x, axis=-1, keepdims=True)
    return x * lax.rsqrt(ms + EPS) * g


def _sigmoid(z):
    return 0.5 * (1.0 + jnp.tanh(0.5 * z))


def _gelu_tanh(x):
    c = math.sqrt(2.0 / math.pi)
    return 0.5 * x * (1.0 + jnp.tanh(c * (x + 0.044715 * (x * x * x))))


def _kv_kernel(mem_ref, g_ref, w_kv_ref, kt_ref, v_ref):
    d = mem_ref.shape[-1]
    m = _rms(mem_ref[...], g_ref[...]).astype(jnp.bfloat16)
    kv = jnp.dot(m, w_kv_ref[...], preferred_element_type=jnp.float32)
    kt_ref[...] = kv[:, :d].T.astype(jnp.bfloat16)
    v_ref[...] = kv[:, d:].astype(jnp.bfloat16)


def _mixer_kernel(x_ref, g_ref, w_in_ref, conv_w_ref, ln_g_ref, ln_b_ref,
                  ws_ref, bst_ref, w_out_ref, o_ref,
                  h_ref, proj_ref, carry_ref, mix_ref):
    tm, d = x_ref.shape
    n_col_chunks = d // CW
    n_row_blocks = tm // CHUNK
    heads_per_chunk = CW // HEAD_DIM

    @pl.when(pl.program_id(1) == 0)
    def _():
        carry_ref[...] = jnp.zeros_like(carry_ref)

    h_ref[...] = _rms(x_ref[...], g_ref[...]).astype(jnp.bfloat16)

    tri = (lax.broadcasted_iota(jnp.int32, (CHUNK, CHUNK), 0)
           >= lax.broadcasted_iota(jnp.int32, (CHUNK, CHUNK), 1))

    for j in range(n_col_chunks):
        cols = slice(j * CW, (j + 1) * CW)
        for s in range(N_SECT):
            proj_ref[:, s * CW:(s + 1) * CW] = jnp.dot(
                h_ref[...], w_in_ref[:, s * d + j * CW:s * d + (j + 1) * CW],
                preferred_element_type=jnp.float32)

        def sect(s, rows):
            return proj_ref[rows, s * CW:(s + 1) * CW]

        w0 = conv_w_ref[0:1, cols]
        w1 = conv_w_ref[1:2, cols]
        w2 = conv_w_ref[2:3, cols]
        wc = [jnp.where(tri, ws_ref[j * heads_per_chunk + hh], 0.0).astype(jnp.bfloat16)
              for hh in range(heads_per_chunk)]

        for r in range(n_row_blocks):
            rows = slice(r * CHUNK, (r + 1) * CHUNK)
            g = sect(1, rows) * sect(2, rows)
            if r == 0:
                prev = carry_ref[:, cols]
            else:
                prows = slice(r * CHUNK - SUBLANES, r * CHUNK)
                prev = sect(1, prows) * sect(2, prows)
            if r == n_row_blocks - 1:
                carry_ref[:, cols] = g[CHUNK - SUBLANES:, :]
            ext = jnp.concatenate([prev, g], axis=0)
            g1 = ext[SUBLANES - 1:SUBLANES - 1 + CHUNK, :]
            g2 = ext[SUBLANES - 2:SUBLANES - 2 + CHUNK, :]
            conv = w0 * g2 + w1 * g1 + w2 * g
            za = sect(3, rows)
            a = sect(0, rows) * conv * (za * _sigmoid(za))
            mix_ref[rows, cols] = a.astype(jnp.bfloat16)

            u = _gelu_tanh(sect(4, rows))
            v = _gelu_tanh(sect(5, rows))
            zb = sect(6, rows)
            sps = []
            for hh in range(heads_per_chunk):
                hc = slice(hh * HEAD_DIM, (hh + 1) * HEAD_DIM)
                gcols = slice(j * CW + hh * HEAD_DIM, j * CW + (hh + 1) * HEAD_DIM)
                vh = v[:, hc]
                mu = jnp.mean(vh, axis=-1, keepdims=True)
                vc = vh - mu
                var = jnp.mean(vc * vc, axis=-1, keepdims=True)
                vn = vc * lax.rsqrt(var + EPS) * ln_g_ref[:, gcols] + ln_b_ref[:, gcols]
                sp = jnp.dot(wc[hh], vn.astype(jnp.bfloat16),
                             preferred_element_type=jnp.float32)
                head = j * heads_per_chunk + hh
                sps.append(sp + bst_ref[:, head:head + 1])
            sp = jnp.concatenate(sps, axis=-1)
            bo = u * sp * (zb * _sigmoid(zb))
            mix_ref[rows, d + j * CW:d + (j + 1) * CW] = bo.astype(jnp.bfloat16)

    o_ref[...] = x_ref[...] + jnp.dot(mix_ref[...], w_out_ref[...],
                                      preferred_element_type=jnp.float32)


def _attn_kernel(x_ref, gx_ref, w_q_ref, kt_ref, v_ref, w_xo_ref, gf_ref, o_ref,
                 att_ref):
    tm, d = x_ref.shape
    hd = d // X_HEADS
    x = x_ref[...]
    h = _rms(x, gx_ref[...]).astype(jnp.bfloat16)
    q = jnp.dot(h, w_q_ref[...], preferred_element_type=jnp.float32)
    q = (q * (1.0 / math.sqrt(hd))).astype(jnp.bfloat16)
    for hh in range(X_HEADS):
        hc = slice(hh * hd, (hh + 1) * hd)
        s = jnp.dot(q[:, hc], kt_ref[hc, :], preferred_element_type=jnp.float32)
        p = jnp.exp(s - jnp.max(s, axis=-1, keepdims=True))
        l = jnp.sum(p, axis=-1, keepdims=True)
        o = jnp.dot(p.astype(jnp.bfloat16), v_ref[:, hc],
                    preferred_element_type=jnp.float32)
        att_ref[:, hc] = (o / l).astype(jnp.bfloat16)
    y = x + jnp.dot(att_ref[...], w_xo_ref[...], preferred_element_type=jnp.float32)
    o_ref[...] = _rms(y, gf_ref[...])


def _resident(shape):
    return pl.BlockSpec(shape, lambda *_: (0,) * len(shape),
                        pipeline_mode=pl.Buffered(1))


def kernel(x, mem, norm_mix_g, w_in, conv_w, gm_ln_g, gm_ln_b, gm_ws, gm_bs, w_out,
           norm_x_g, norm_mem_g, w_q, w_kv, w_xo, norm_final_g):
    b, s, d = x.shape
    m_len = mem.shape[1]
    assert w_in.shape[0] == 1, "the final norm is fused into the (single) layer's attention call"
    assert s % TM == 0 and TM % CHUNK == 0 and d % CW == 0
    bf16 = jnp.bfloat16
    params = pltpu.CompilerParams(
        dimension_semantics=("arbitrary", "arbitrary"), vmem_limit_bytes=VMEM_LIMIT)

    for l in range(1):
        w_in_c = w_in[l].astype(bf16)
        row = lambda a: a.reshape(1, -1)

        kt, v = pl.pallas_call(
            _kv_kernel,
            grid=(b,),
            in_specs=[pl.BlockSpec((None, m_len, d), lambda i: (i, 0, 0)),
                      _resident((1, d)), _resident((d, 2 * d))],
            out_specs=[pl.BlockSpec((None, d, m_len), lambda i: (i, 0, 0)),
                       pl.BlockSpec((None, m_len, d), lambda i: (i, 0, 0))],
            out_shape=[jax.ShapeDtypeStruct((b, d, m_len), bf16),
                       jax.ShapeDtypeStruct((b, m_len, d), bf16)],
            compiler_params=pltpu.CompilerParams(
                dimension_semantics=("arbitrary",), vmem_limit_bytes=VMEM_LIMIT),
            name="kv_proj",
        )(mem, row(norm_mem_g[l]), w_kv[l].astype(bf16))

        x = pl.pallas_call(
            _mixer_kernel,
            grid=(b, s // TM),
            in_specs=[pl.BlockSpec((None, TM, d), lambda i, t: (i, t, 0)),
                      _resident((1, d)),
                      _resident((d, N_SECT * d)),
                      _resident((CONV_K, d)),
                      _resident((1, d)), _resident((1, d)),
                      _resident(gm_ws.shape[1:]),
                      _resident((CHUNK, gm_bs.shape[1])),
                      _resident((2 * d, d))],
            out_specs=pl.BlockSpec((None, TM, d), lambda i, t: (i, t, 0)),
            out_shape=jax.ShapeDtypeStruct((b, s, d), jnp.float32),
            scratch_shapes=[pltpu.VMEM((TM, d), bf16),
                            pltpu.VMEM((TM, N_SECT * CW), jnp.float32),
                            pltpu.VMEM((SUBLANES, d), jnp.float32),
                            pltpu.VMEM((TM, 2 * d), bf16)],
            compiler_params=params,
            name="mixer",
        )(x, row(norm_mix_g[l]), w_in_c, conv_w[l], row(gm_ln_g[l]), row(gm_ln_b[l]),
          gm_ws[l], gm_bs[l].T, w_out[l].astype(bf16))

        x = pl.pallas_call(
            _attn_kernel,
            grid=(b, s // TM),
            in_specs=[pl.BlockSpec((None, TM, d), lambda i, t: (i, t, 0)),
                      _resident((1, d)),
                      _resident((d, d)),
                      pl.BlockSpec((None, d, m_len), lambda i, t: (i, 0, 0)),
                      pl.BlockSpec((None, m_len, d), lambda i, t: (i, 0, 0)),
                      _resident((d, d)),
                      _resident((1, d))],
            out_specs=pl.BlockSpec((None, TM, d), lambda i, t: (i, t, 0)),
            out_shape=jax.ShapeDtypeStruct((b, s, d), jnp.float32),
            scratch_shapes=[pltpu.VMEM((TM, d), bf16)],
            compiler_params=params,
            name="xattn",
        )(x, row(norm_x_g[l]), w_q[l].astype(bf16), kt, v, w_xo[l].astype(bf16),
          row(norm_final_g))
    return x
```

```python
import functools
import math

import jax
import jax.numpy as jnp
from jax import lax
from jax.experimental import pallas as pl
from jax.experimental.pallas import tpu as pltpu

EPS = 1e-6
CONV_K = 3
CHUNK = 128
HEAD_DIM = 128
N_SECT = 7
X_HEADS = 4
SUBLANES = 8

TM = 512
CW = 256
VMEM_LIMIT = 56 * 1024 * 1024


def _rms(x, g):
    ms = jnp.mean(x * x, axis=-1, keepdims=True)
    return x * lax.rsqrt(ms + EPS) * g


def _sigmoid(z):
    return 0.5 * (1.0 + jnp.tanh(0.5 * z))


def _gelu_tanh(x):
    c = math.sqrt(2.0 / math.pi)
    return 0.5 * x * (1.0 + jnp.tanh(c * (x + 0.044715 * (x * x * x))))


def _kv_kernel(mem_ref, g_ref, w_kv_ref, kt_ref, v_ref):
    d = mem_ref.shape[-1]
    m = _rms(mem_ref[...], g_ref[...]).astype(jnp.bfloat16)
    kv = jnp.dot(m, w_kv_ref[...], preferred_element_type=jnp.float32)
    kt_ref[...] = kv[:, :d].T.astype(jnp.bfloat16)
    v_ref[...] = kv[:, d:].astype(jnp.bfloat16)


def _mixer_kernel(x_ref, g_ref, w_in_ref, conv_w_ref, ln_g_ref, ln_b_ref,
                  ws_ref, bst_ref, w_out_ref, o_ref,
                  h_ref, proj_ref, carry_ref, mix_ref):
    tm, d = x_ref.shape
    n_col_chunks = d // CW
    n_row_blocks = tm // CHUNK
    heads_per_chunk = CW // HEAD_DIM

    @pl.when(pl.program_id(1) == 0)
    def _():
        carry_ref[...] = jnp.zeros_like(carry_ref)

    h_ref[...] = _rms(x_ref[...], g_ref[...]).astype(jnp.bfloat16)

    tri = (lax.broadcasted_iota(jnp.int32, (CHUNK, CHUNK), 0)
           >= lax.broadcasted_iota(jnp.int32, (CHUNK, CHUNK), 1))

    def in_proj(j):
        for s in range(N_SECT):
            proj_ref[j % 2, :, s * CW:(s + 1) * CW] = jnp.dot(
                h_ref[...], w_in_ref[:, s * d + j * CW:s * d + (j + 1) * CW],
                preferred_element_type=jnp.float32)

    in_proj(0)
    for j in range(n_col_chunks):
        cols = slice(j * CW, (j + 1) * CW)
        if j + 1 < n_col_chunks:
            in_proj(j + 1)

        def sect(s, rows, j=j):
            return proj_ref[j % 2, rows, s * CW:(s + 1) * CW]

        w0 = conv_w_ref[0:1, cols]
        w1 = conv_w_ref[1:2, cols]
        w2 = conv_w_ref[2:3, cols]
        wc = [jnp.where(tri, ws_ref[j * heads_per_chunk + hh], 0.0).astype(jnp.bfloat16)
              for hh in range(heads_per_chunk)]

        for r in range(n_row_blocks):
            rows = slice(r * CHUNK, (r + 1) * CHUNK)
            g = sect(1, rows) * sect(2, rows)
            if r == 0:
                prev = carry_ref[:, cols]
            else:
                prows = slice(r * CHUNK - SUBLANES, r * CHUNK)
                prev = sect(1, prows) * sect(2, prows)
            if r == n_row_blocks - 1:
                carry_ref[:, cols] = g[CHUNK - SUBLANES:, :]
            ext = jnp.concatenate([prev, g], axis=0)
            g1 = ext[SUBLANES - 1:SUBLANES - 1 + CHUNK, :]
            g2 = ext[SUBLANES - 2:SUBLANES - 2 + CHUNK, :]
            conv = w0 * g2 + w1 * g1 + w2 * g
            za = sect(3, rows)
            a = sect(0, rows) * conv * (za * _sigmoid(za))
            mix_ref[rows, cols] = a.astype(jnp.bfloat16)

            u = _gelu_tanh(sect(4, rows))
            v = _gelu_tanh(sect(5, rows))
            zb = sect(6, rows)
            sps = []
            for hh in range(heads_per_chunk):
                hc = slice(hh * HEAD_DIM, (hh + 1) * HEAD_DIM)
                gcols = slice(j * CW + hh * HEAD_DIM, j * CW + (hh + 1) * HEAD_DIM)
                vh = v[:, hc]
                mu = jnp.mean(vh, axis=-1, keepdims=True)
                vc = vh - mu
                var = jnp.mean(vc * vc, axis=-1, keepdims=True)
                vn = vc * lax.rsqrt(var + EPS) * ln_g_ref[:, gcols] + ln_b_ref[:, gcols]
                sp = jnp.dot(wc[hh], vn.astype(jnp.bfloat16),
                             preferred_element_type=jnp.float32)
                head = j * heads_per_chunk + hh
                sps.append(sp + bst_ref[:, head:head + 1])
            sp = jnp.concatenate(sps, axis=-1)
            bo = u * sp * (zb * _sigmoid(zb))
            mix_ref[rows, d + j * CW:d + (j + 1) * CW] = bo.astype(jnp.bfloat16)

    o_ref[...] = x_ref[...] + jnp.dot(mix_ref[...], w_out_ref[...],
                                      preferred_element_type=jnp.float32)


def _attn_kernel(x_ref, gx_ref, w_q_ref, kt_ref, v_ref, w_xo_ref, gf_ref, o_ref,
                 att_ref):
    tm, d = x_ref.shape
    hd = d // X_HEADS
    x = x_ref[...]
    h = _rms(x, gx_ref[...]).astype(jnp.bfloat16)
    q = jnp.dot(h, w_q_ref[...], preferred_element_type=jnp.float32)
    q = (q * (1.0 / math.sqrt(hd))).astype(jnp.bfloat16)
    for hh in range(X_HEADS):
        hc = slice(hh * hd, (hh + 1) * hd)
        s = jnp.dot(q[:, hc], kt_ref[hc, :], preferred_element_type=jnp.float32)
        p = jnp.exp(s - jnp.max(s, axis=-1, keepdims=True))
        l = jnp.sum(p, axis=-1, keepdims=True)
        o = jnp.dot(p.astype(jnp.bfloat16), v_ref[:, hc],
                    preferred_element_type=jnp.float32)
        att_ref[:, hc] = (o / l).astype(jnp.bfloat16)
    y = x + jnp.dot(att_ref[...], w_xo_ref[...], preferred_element_type=jnp.float32)
    o_ref[...] = _rms(y, gf_ref[...])


def _resident(shape):
    return pl.BlockSpec(shape, lambda *_: (0,) * len(shape),
                        pipeline_mode=pl.Buffered(1))


def kernel(x, mem, norm_mix_g, w_in, conv_w, gm_ln_g, gm_ln_b, gm_ws, gm_bs, w_out,
           norm_x_g, norm_mem_g, w_q, w_kv, w_xo, norm_final_g):
    b, s, d = x.shape
    m_len = mem.shape[1]
    assert w_in.shape[0] == 1, "the final norm is fused into the (single) layer's attention call"
    assert s % TM == 0 and TM % CHUNK == 0 and d % CW == 0
    bf16 = jnp.bfloat16
    params = pltpu.CompilerParams(
        dimension_semantics=("arbitrary", "arbitrary"), vmem_limit_bytes=VMEM_LIMIT)

    for l in range(1):
        w_in_c = w_in[l].astype(bf16)
        row = lambda a: a.reshape(1, -1)

        kt, v = pl.pallas_call(
            _kv_kernel,
            grid=(b,),
            in_specs=[pl.BlockSpec((None, m_len, d), lambda i: (i, 0, 0)),
                      _resident((1, d)), _resident((d, 2 * d))],
            out_specs=[pl.BlockSpec((None, d, m_len), lambda i: (i, 0, 0)),
                       pl.BlockSpec((None, m_len, d), lambda i: (i, 0, 0))],
            out_shape=[jax.ShapeDtypeStruct((b, d, m_len), bf16),
                       jax.ShapeDtypeStruct((b, m_len, d), bf16)],
            compiler_params=pltpu.CompilerParams(
                dimension_semantics=("arbitrary",), vmem_limit_bytes=VMEM_LIMIT),
            name="kv_proj",
        )(mem, row(norm_mem_g[l]), w_kv[l].astype(bf16))

        x = pl.pallas_call(
            _mixer_kernel,
            grid=(b, s // TM),
            in_specs=[pl.BlockSpec((None, TM, d), lambda i, t: (i, t, 0)),
                      _resident((1, d)),
                      _resident((d, N_SECT * d)),
                      _resident((CONV_K, d)),
                      _resident((1, d)), _resident((1, d)),
                      _resident(gm_ws.shape[1:]),
                      _resident((CHUNK, gm_bs.shape[1])),
                      _resident((2 * d, d))],
            out_specs=pl.BlockSpec((None, TM, d), lambda i, t: (i, t, 0)),
            out_shape=jax.ShapeDtypeStruct((b, s, d), jnp.float32),
            scratch_shapes=[pltpu.VMEM((TM, d), bf16),
                            pltpu.VMEM((2, TM, N_SECT * CW), jnp.float32),
                            pltpu.VMEM((SUBLANES, d), jnp.float32),
                            pltpu.VMEM((TM, 2 * d), bf16)],
            compiler_params=params,
            name="mixer",
        )(x, row(norm_mix_g[l]), w_in_c, conv_w[l], row(gm_ln_g[l]), row(gm_ln_b[l]),
          gm_ws[l], gm_bs[l].T, w_out[l].astype(bf16))

        x = pl.pallas_call(
            _attn_kernel,
            grid=(b, s // TM),
            in_specs=[pl.BlockSpec((None, TM, d), lambda i, t: (i, t, 0)),
                      _resident((1, d)),
                      _resident((d, d)),
                      pl.BlockSpec((None, d, m_len), lambda i, t: (i, 0, 0)),
                      pl.BlockSpec((None, m_len, d), lambda i, t: (i, 0, 0)),
                      _resident((d, d)),
                      _resident((1, d))],
            out_specs=pl.BlockSpec((None, TM, d), lambda i, t: (i, t, 0)),
            out_shape=jax.ShapeDtypeStruct((b, s, d), jnp.float32),
            scratch_shapes=[pltpu.VMEM((TM, d), bf16)],
            compiler_params=params,
            name="xattn",
        )(x, row(norm_x_g[l]), w_q[l].astype(bf16), kt, v, w_xo[l].astype(bf16),
          row(norm_final_g))
    return x
```

```python
import math

import jax
import jax.numpy as jnp
from jax import lax
from jax.experimental import pallas as pl
from jax.experimental.pallas import tpu as pltpu

EPS = 1e-6
CONV_K = 3
CHUNK = 128
HEAD_DIM = 128
N_SECT = 7
X_HEADS = 4
SUBLANES = 8

TM = 512
CW = 256
W_SLAB = (256, 1024)
VMEM_LIMIT = 56 * 1024 * 1024


def _rms(x, g):
    ms = jnp.mean(x * x, axis=-1, keepdims=True)
    return x * lax.rsqrt(ms + EPS) * g


def _sigmoid(z):
    return 0.5 * (1.0 + jnp.tanh(0.5 * z))


def _gelu_tanh(x):
    c = math.sqrt(2.0 / math.pi)
    return 0.5 * x * (1.0 + jnp.tanh(c * (x + 0.044715 * (x * x * x))))


def _load_weights_bf16(pairs, stage_ref, sem_ref):
    slab_r, slab_c = stage_ref.shape[1:]
    blocks = [(src, dst, r, c)
              for src, dst in pairs
              for c in range(0, src.shape[1], slab_c)
              for r in range(0, src.shape[0], slab_r)]

    def copy(k):
        src, _, r, c = blocks[k]
        return pltpu.make_async_copy(src.at[pl.ds(r, slab_r), pl.ds(c, slab_c)],
                                     stage_ref.at[k % 2], sem_ref.at[k % 2])

    copy(0).start()
    for k, (_, dst, r, c) in enumerate(blocks):
        if k + 1 < len(blocks):
            copy(k + 1).start()
        copy(k).wait()
        dst[r:r + slab_r, c:c + slab_c] = stage_ref[k % 2].astype(jnp.bfloat16)


def _first_grid_step():
    return (pl.program_id(0) == 0) & (pl.program_id(1) == 0)


def _mixer_kernel(x_ref, g_ref, w_in_hbm, conv_w_ref, ln_g_ref, ln_b_ref,
                  ws_ref, bst_ref, w_out_hbm, o_ref,
                  w_in_ref, w_out_ref, stage_ref, sem_ref,
                  h_ref, proj_ref, carry_ref, mix_ref):
    tm, d = x_ref.shape
    n_col_chunks = d // CW
    n_row_blocks = tm // CHUNK
    heads_per_chunk = CW // HEAD_DIM

    @pl.when(_first_grid_step())
    def _():
        _load_weights_bf16([(w_in_hbm, w_in_ref), (w_out_hbm, w_out_ref)],
                           stage_ref, sem_ref)

    @pl.when(pl.program_id(1) == 0)
    def _():
        carry_ref[...] = jnp.zeros_like(carry_ref)

    h_ref[...] = _rms(x_ref[...], g_ref[...]).astype(jnp.bfloat16)

    tri = (lax.broadcasted_iota(jnp.int32, (CHUNK, CHUNK), 0)
           >= lax.broadcasted_iota(jnp.int32, (CHUNK, CHUNK), 1))

    def in_proj(j):
        for s in range(N_SECT):
            proj_ref[j % 2, :, s * CW:(s + 1) * CW] = jnp.dot(
                h_ref[...], w_in_ref[:, s * d + j * CW:s * d + (j + 1) * CW],
                preferred_element_type=jnp.float32)

    in_proj(0)
    for j in range(n_col_chunks):
        cols = slice(j * CW, (j + 1) * CW)
        if j + 1 < n_col_chunks:
            in_proj(j + 1)

        def sect(s, rows, j=j):
            return proj_ref[j % 2, rows, s * CW:(s + 1) * CW]

        w0 = conv_w_ref[0:1, cols]
        w1 = conv_w_ref[1:2, cols]
        w2 = conv_w_ref[2:3, cols]
        wc = [jnp.where(tri, ws_ref[j * heads_per_chunk + hh], 0.0).astype(jnp.bfloat16)
              for hh in range(heads_per_chunk)]

        for r in range(n_row_blocks):
            rows = slice(r * CHUNK, (r + 1) * CHUNK)
            g = sect(1, rows) * sect(2, rows)
            if r == 0:
                prev = carry_ref[:, cols]
            else:
                prows = slice(r * CHUNK - SUBLANES, r * CHUNK)
                prev = sect(1, prows) * sect(2, prows)
            if r == n_row_blocks - 1:
                carry_ref[:, cols] = g[CHUNK - SUBLANES:, :]
            ext = jnp.concatenate([prev, g], axis=0)
            g1 = ext[SUBLANES - 1:SUBLANES - 1 + CHUNK, :]
            g2 = ext[SUBLANES - 2:SUBLANES - 2 + CHUNK, :]
            conv = w0 * g2 + w1 * g1 + w2 * g
            za = sect(3, rows)
            a = sect(0, rows) * conv * (za * _sigmoid(za))
            mix_ref[rows, cols] = a.astype(jnp.bfloat16)

            u = _gelu_tanh(sect(4, rows))
            v = _gelu_tanh(sect(5, rows))
            zb = sect(6, rows)
            sps = []
            for hh in range(heads_per_chunk):
                hc = slice(hh * HEAD_DIM, (hh + 1) * HEAD_DIM)
                gcols = slice(j * CW + hh * HEAD_DIM, j * CW + (hh + 1) * HEAD_DIM)
                vh = v[:, hc]
                mu = jnp.mean(vh, axis=-1, keepdims=True)
                vc = vh - mu
                var = jnp.mean(vc * vc, axis=-1, keepdims=True)
                vn = vc * lax.rsqrt(var + EPS) * ln_g_ref[:, gcols] + ln_b_ref[:, gcols]
                sp = jnp.dot(wc[hh], vn.astype(jnp.bfloat16),
                             preferred_element_type=jnp.float32)
                head = j * heads_per_chunk + hh
                sps.append(sp + bst_ref[:, head:head + 1])
            sp = jnp.concatenate(sps, axis=-1)
            bo = u * sp * (zb * _sigmoid(zb))
            mix_ref[rows, d + j * CW:d + (j + 1) * CW] = bo.astype(jnp.bfloat16)

    o_ref[...] = x_ref[...] + jnp.dot(mix_ref[...], w_out_ref[...],
                                      preferred_element_type=jnp.float32)


def _attn_kernel(x_ref, mem_ref, gx_ref, gm_ref, w_q_hbm, w_kv_hbm, w_xo_hbm, gf_ref,
                 o_ref,
                 w_q_ref, w_kv_ref, w_xo_ref, stage_ref, sem_ref,
                 kt_ref, v_ref, att_ref):
    tm, d = x_ref.shape
    hd = d // X_HEADS

    @pl.when(_first_grid_step())
    def _():
        _load_weights_bf16([(w_q_hbm, w_q_ref), (w_kv_hbm, w_kv_ref), (w_xo_hbm, w_xo_ref)],
                           stage_ref, sem_ref)

    @pl.when(pl.program_id(1) == 0)
    def _():
        m = _rms(mem_ref[...], gm_ref[...]).astype(jnp.bfloat16)
        kv = jnp.dot(m, w_kv_ref[...], preferred_element_type=jnp.float32)
        kt_ref[...] = kv[:, :d].T.astype(jnp.bfloat16)
        v_ref[...] = kv[:, d:].astype(jnp.bfloat16)

    x = x_ref[...]
    h = _rms(x, gx_ref[...]).astype(jnp.bfloat16)
    q = jnp.dot(h, w_q_ref[...], preferred_element_type=jnp.float32)
    q = (q * (1.0 / math.sqrt(hd))).astype(jnp.bfloat16)
    for hh in range(X_HEADS):
        hc = slice(hh * hd, (hh + 1) * hd)
        s = jnp.dot(q[:, hc], kt_ref[hc, :], preferred_element_type=jnp.float32)
        p = jnp.exp(s - jnp.max(s, axis=-1, keepdims=True))
        l = jnp.sum(p, axis=-1, keepdims=True)
        o = jnp.dot(p.astype(jnp.bfloat16), v_ref[:, hc],
                    preferred_element_type=jnp.float32)
        att_ref[:, hc] = (o / l).astype(jnp.bfloat16)
    y = x + jnp.dot(att_ref[...], w_xo_ref[...], preferred_element_type=jnp.float32)
    o_ref[...] = _rms(y, gf_ref[...])


def _resident(shape):
    return pl.BlockSpec(shape, lambda *_: (0,) * len(shape),
                        pipeline_mode=pl.Buffered(1))


def kernel(x, mem, norm_mix_g, w_in, conv_w, gm_ln_g, gm_ln_b, gm_ws, gm_bs, w_out,
           norm_x_g, norm_mem_g, w_q, w_kv, w_xo, norm_final_g):
    b, s, d = x.shape
    m_len = mem.shape[1]
    assert w_in.shape[0] == 1, "the final norm is fused into the (single) layer's attention call"
    assert s % TM == 0 and TM % CHUNK == 0 and d % CW == 0
    assert d % W_SLAB[0] == 0 and d % W_SLAB[1] == 0
    bf16 = jnp.bfloat16
    f32 = jnp.float32
    params = pltpu.CompilerParams(
        dimension_semantics=("arbitrary", "arbitrary"), vmem_limit_bytes=VMEM_LIMIT)
    row = lambda a: a.reshape(1, -1)
    hbm = pl.BlockSpec(memory_space=pl.ANY)
    tile = pl.BlockSpec((None, TM, d), lambda i, t: (i, t, 0))
    staging = [pltpu.VMEM((2,) + W_SLAB, f32), pltpu.SemaphoreType.DMA((2,))]

    x = pl.pallas_call(
        _mixer_kernel,
        grid=(b, s // TM),
        in_specs=[tile,
                  _resident((1, d)),
                  hbm,
                  _resident((CONV_K, d)),
                  _resident((1, d)), _resident((1, d)),
                  _resident(gm_ws.shape[1:]),
                  _resident((CHUNK, gm_bs.shape[1])),
                  hbm],
        out_specs=tile,
        out_shape=jax.ShapeDtypeStruct((b, s, d), f32),
        scratch_shapes=[pltpu.VMEM((d, N_SECT * d), bf16),
                        pltpu.VMEM((2 * d, d), bf16),
                        *staging,
                        pltpu.VMEM((TM, d), bf16),
                        pltpu.VMEM((2, TM, N_SECT * CW), f32),
                        pltpu.VMEM((SUBLANES, d), f32),
                        pltpu.VMEM((TM, 2 * d), bf16)],
        compiler_params=params,
        name="mixer",
    )(x, row(norm_mix_g[0]), w_in[0], conv_w[0], row(gm_ln_g[0]), row(gm_ln_b[0]),
      gm_ws[0], gm_bs[0].T, w_out[0])

    return pl.pallas_call(
        _attn_kernel,
        grid=(b, s // TM),
        in_specs=[tile,
                  pl.BlockSpec((None, m_len, d), lambda i, t: (i, 0, 0)),
                  _resident((1, d)), _resident((1, d)),
                  hbm, hbm, hbm,
                  _resident((1, d))],
        out_specs=tile,
        out_shape=jax.ShapeDtypeStruct((b, s, d), f32),
        scratch_shapes=[pltpu.VMEM((d, d), bf16),
                        pltpu.VMEM((d, 2 * d), bf16),
                        pltpu.VMEM((d, d), bf16),
                        *staging,
                        pltpu.VMEM((d, m_len), bf16),
                        pltpu.VMEM((m_len, d), bf16),
                        pltpu.VMEM((TM, d), bf16)],
        compiler_params=params,
        name="xattn",
    )(x, mem, row(norm_x_g[0]), row(norm_mem_g[0]), w_q[0], w_kv[0], w_xo[0],
      row(norm_final_g))
```

```python
import math

import jax
import jax.numpy as jnp
from jax import lax
from jax.experimental import pallas as pl
from jax.experimental.pallas import tpu as pltpu

EPS = 1e-6
CONV_K = 3
CHUNK = 128
HEAD_DIM = 128
N_SECT = 7
X_HEADS = 4
SUBLANES = 8

TM = 512
CW = 256
N_STAGE = 4
STAGE_BYTES = 1 << 20
VMEM_LIMIT = 56 * 1024 * 1024


def _rms(x, g):
    ms = jnp.mean(x * x, axis=-1, keepdims=True)
    return x * lax.rsqrt(ms + EPS) * g


def _sigmoid(z):
    return 0.5 * (1.0 + jnp.tanh(0.5 * z))


def _gelu_tanh(x):
    c = math.sqrt(2.0 / math.pi)
    return 0.5 * x * (1.0 + jnp.tanh(c * (x + 0.044715 * (x * x * x))))


def _load_weights_bf16(jobs, sem_ref):
    blocks = [(src, dst, stage, r)
              for src, dst, stage in jobs
              for r in range(0, src.shape[0], stage.shape[1])]

    def copy(k):
        src, _, stage, r = blocks[k]
        return pltpu.make_async_copy(src.at[pl.ds(r, stage.shape[1])],
                                     stage.at[k % N_STAGE], sem_ref.at[k % N_STAGE])

    for k in range(min(N_STAGE - 1, len(blocks))):
        copy(k).start()
    for k, (_, dst, stage, r) in enumerate(blocks):
        if k + N_STAGE - 1 < len(blocks):
            copy(k + N_STAGE - 1).start()
        copy(k).wait()
        dst[r:r + stage.shape[1], :] = stage[k % N_STAGE].astype(jnp.bfloat16)


def _first_grid_step():
    return (pl.program_id(0) == 0) & (pl.program_id(1) == 0)


def _mixer_kernel(x_ref, g_ref, w_in_hbm, conv_w_ref, ln_g_ref, ln_b_ref,
                  ws_ref, bst_ref, w_out_hbm, o_ref,
                  w_in_ref, w_out_ref, stage_in_ref, stage_out_ref, sem_ref,
                  h_ref, proj_ref, carry_ref, mix_ref):
    tm, d = x_ref.shape
    n_col_chunks = d // CW
    n_row_blocks = tm // CHUNK
    heads_per_chunk = CW // HEAD_DIM

    @pl.when(_first_grid_step())
    def _():
        _load_weights_bf16([(w_in_hbm, w_in_ref, stage_in_ref),
                            (w_out_hbm, w_out_ref, stage_out_ref)], sem_ref)

    @pl.when(pl.program_id(1) == 0)
    def _():
        carry_ref[...] = jnp.zeros_like(carry_ref)

    h_ref[...] = _rms(x_ref[...], g_ref[...]).astype(jnp.bfloat16)

    tri = (lax.broadcasted_iota(jnp.int32, (CHUNK, CHUNK), 0)
           >= lax.broadcasted_iota(jnp.int32, (CHUNK, CHUNK), 1))

    def in_proj(j):
        for s in range(N_SECT):
            proj_ref[j % 2, :, s * CW:(s + 1) * CW] = jnp.dot(
                h_ref[...], w_in_ref[:, s * d + j * CW:s * d + (j + 1) * CW],
                preferred_element_type=jnp.float32)

    in_proj(0)
    for j in range(n_col_chunks):
        cols = slice(j * CW, (j + 1) * CW)
        if j + 1 < n_col_chunks:
            in_proj(j + 1)

        def sect(s, rows, j=j):
            return proj_ref[j % 2, rows, s * CW:(s + 1) * CW]

        w0 = conv_w_ref[0:1, cols]
        w1 = conv_w_ref[1:2, cols]
        w2 = conv_w_ref[2:3, cols]
        wc = [jnp.where(tri, ws_ref[j * heads_per_chunk + hh], 0.0).astype(jnp.bfloat16)
              for hh in range(heads_per_chunk)]

        for r in range(n_row_blocks):
            rows = slice(r * CHUNK, (r + 1) * CHUNK)
            g = sect(1, rows) * sect(2, rows)
            if r == 0:
                prev = carry_ref[:, cols]
            else:
                prows = slice(r * CHUNK - SUBLANES, r * CHUNK)
                prev = sect(1, prows) * sect(2, prows)
            if r == n_row_blocks - 1:
                carry_ref[:, cols] = g[CHUNK - SUBLANES:, :]
            ext = jnp.concatenate([prev, g], axis=0)
            g1 = ext[SUBLANES - 1:SUBLANES - 1 + CHUNK, :]
            g2 = ext[SUBLANES - 2:SUBLANES - 2 + CHUNK, :]
            conv = w0 * g2 + w1 * g1 + w2 * g
            za = sect(3, rows)
            a = sect(0, rows) * conv * (za * _sigmoid(za))
            mix_ref[rows, cols] = a.astype(jnp.bfloat16)

            u = _gelu_tanh(sect(4, rows))
            v = _gelu_tanh(sect(5, rows))
            zb = sect(6, rows)
            sps = []
            for hh in range(heads_per_chunk):
                hc = slice(hh * HEAD_DIM, (hh + 1) * HEAD_DIM)
                gcols = slice(j * CW + hh * HEAD_DIM, j * CW + (hh + 1) * HEAD_DIM)
                vh = v[:, hc]
                mu = jnp.mean(vh, axis=-1, keepdims=True)
                vc = vh - mu
                var = jnp.mean(vc * vc, axis=-1, keepdims=True)
                vn = vc * lax.rsqrt(var + EPS) * ln_g_ref[:, gcols] + ln_b_ref[:, gcols]
                sp = jnp.dot(wc[hh], vn.astype(jnp.bfloat16),
                             preferred_element_type=jnp.float32)
                head = j * heads_per_chunk + hh
                sps.append(sp + bst_ref[:, head:head + 1])
            sp = jnp.concatenate(sps, axis=-1)
            bo = u * sp * (zb * _sigmoid(zb))
            mix_ref[rows, d + j * CW:d + (j + 1) * CW] = bo.astype(jnp.bfloat16)

    o_ref[...] = x_ref[...] + jnp.dot(mix_ref[...], w_out_ref[...],
                                      preferred_element_type=jnp.float32)


def _attn_kernel(x_ref, mem_ref, gx_ref, gm_ref, w_q_hbm, w_kv_hbm, w_xo_hbm, gf_ref,
                 o_ref,
                 w_q_ref, w_kv_ref, w_xo_ref, stage_d_ref, stage_kv_ref, sem_ref,
                 kt_ref, v_ref, att_ref):
    tm, d = x_ref.shape
    hd = d // X_HEADS

    @pl.when(_first_grid_step())
    def _():
        _load_weights_bf16([(w_q_hbm, w_q_ref, stage_d_ref),
                            (w_kv_hbm, w_kv_ref, stage_kv_ref),
                            (w_xo_hbm, w_xo_ref, stage_d_ref)], sem_ref)

    @pl.when(pl.program_id(1) == 0)
    def _():
        m = _rms(mem_ref[...], gm_ref[...]).astype(jnp.bfloat16)
        kv = jnp.dot(m, w_kv_ref[...], preferred_element_type=jnp.float32)
        kt_ref[...] = kv[:, :d].T.astype(jnp.bfloat16)
        v_ref[...] = kv[:, d:].astype(jnp.bfloat16)

    x = x_ref[...]
    h = _rms(x, gx_ref[...]).astype(jnp.bfloat16)
    q = jnp.dot(h, w_q_ref[...], preferred_element_type=jnp.float32)
    q = (q * (1.0 / math.sqrt(hd))).astype(jnp.bfloat16)
    for hh in range(X_HEADS):
        hc = slice(hh * hd, (hh + 1) * hd)
        s = jnp.dot(q[:, hc], kt_ref[hc, :], preferred_element_type=jnp.float32)
        p = jnp.exp(s - jnp.max(s, axis=-1, keepdims=True))
        l = jnp.sum(p, axis=-1, keepdims=True)
        o = jnp.dot(p.astype(jnp.bfloat16), v_ref[:, hc],
                    preferred_element_type=jnp.float32)
        att_ref[:, hc] = (o / l).astype(jnp.bfloat16)
    y = x + jnp.dot(att_ref[...], w_xo_ref[...], preferred_element_type=jnp.float32)
    o_ref[...] = _rms(y, gf_ref[...])


def _stage(cols):
    rows = 1 << int(math.log2(STAGE_BYTES // (4 * cols)))
    return pltpu.VMEM((N_STAGE, rows, cols), jnp.float32)


def _resident(shape):
    return pl.BlockSpec(shape, lambda *_: (0,) * len(shape),
                        pipeline_mode=pl.Buffered(1))


def kernel(x, mem, norm_mix_g, w_in, conv_w, gm_ln_g, gm_ln_b, gm_ws, gm_bs, w_out,
           norm_x_g, norm_mem_g, w_q, w_kv, w_xo, norm_final_g):
    b, s, d = x.shape
    m_len = mem.shape[1]
    assert w_in.shape[0] == 1, "the final norm is fused into the (single) layer's attention call"
    assert s % TM == 0 and TM % CHUNK == 0 and d % CW == 0
    bf16 = jnp.bfloat16
    f32 = jnp.float32
    params = pltpu.CompilerParams(
        dimension_semantics=("arbitrary", "arbitrary"), vmem_limit_bytes=VMEM_LIMIT)
    row = lambda a: a.reshape(1, -1)
    hbm = pl.BlockSpec(memory_space=pl.ANY)
    tile = pl.BlockSpec((None, TM, d), lambda i, t: (i, t, 0))
    sems = pltpu.SemaphoreType.DMA((N_STAGE,))

    x = pl.pallas_call(
        _mixer_kernel,
        grid=(b, s // TM),
        in_specs=[tile,
                  _resident((1, d)),
                  hbm,
                  _resident((CONV_K, d)),
                  _resident((1, d)), _resident((1, d)),
                  _resident(gm_ws.shape[1:]),
                  _resident((CHUNK, gm_bs.shape[1])),
                  hbm],
        out_specs=tile,
        out_shape=jax.ShapeDtypeStruct((b, s, d), f32),
        scratch_shapes=[pltpu.VMEM((d, N_SECT * d), bf16),
                        pltpu.VMEM((2 * d, d), bf16),
                        _stage(N_SECT * d), _stage(d), sems,
                        pltpu.VMEM((TM, d), bf16),
                        pltpu.VMEM((2, TM, N_SECT * CW), f32),
                        pltpu.VMEM((SUBLANES, d), f32),
                        pltpu.VMEM((TM, 2 * d), bf16)],
        compiler_params=params,
        name="mixer",
    )(x, row(norm_mix_g[0]), w_in[0], conv_w[0], row(gm_ln_g[0]), row(gm_ln_b[0]),
      gm_ws[0], gm_bs[0].T, w_out[0])

    return pl.pallas_call(
        _attn_kernel,
        grid=(b, s // TM),
        in_specs=[tile,
                  pl.BlockSpec((None, m_len, d), lambda i, t: (i, 0, 0)),
                  _resident((1, d)), _resident((1, d)),
                  hbm, hbm, hbm,
                  _resident((1, d))],
        out_specs=tile,
        out_shape=jax.ShapeDtypeStruct((b, s, d), f32),
        scratch_shapes=[pltpu.VMEM((d, d), bf16),
                        pltpu.VMEM((d, 2 * d), bf16),
                        pltpu.VMEM((d, d), bf16),
                        _stage(d), _stage(2 * d), sems,
                        pltpu.VMEM((d, m_len), bf16),
                        pltpu.VMEM((m_len, d), bf16),
                        pltpu.VMEM((TM, d), bf16)],
        compiler_params=params,
        name="xattn",
    )(x, mem, row(norm_x_g[0]), row(norm_mem_g[0]), w_q[0], w_kv[0], w_xo[0],
      row(norm_final_g))
```

```python
import math

import jax
import jax.numpy as jnp
from jax import lax
from jax.experimental import pallas as pl
from jax.experimental.pallas import tpu as pltpu

EPS = 1e-6
CONV_K = 3
CHUNK = 128
HEAD_DIM = 128
N_SECT = 7
X_HEADS = 4
SUBLANES = 8

TM = 512
TM_ATTN = 1024
ATTN_HALF = 512
CW = 256
N_STAGE = 4
STAGE_BYTES = 1 << 20
VMEM_LIMIT = 56 * 1024 * 1024


def _rms(x, g):
    ms = jnp.mean(x * x, axis=-1, keepdims=True)
    return x * lax.rsqrt(ms + EPS) * g


def _sigmoid(z):
    return 0.5 * (1.0 + jnp.tanh(0.5 * z))


def _gelu_tanh(x):
    c = math.sqrt(2.0 / math.pi)
    return 0.5 * x * (1.0 + jnp.tanh(c * (x + 0.044715 * (x * x * x))))


def _load_weights_bf16(jobs, sem_ref):
    blocks = [(src, dst, stage, r)
              for src, dst, stage in jobs
              for r in range(0, src.shape[0], stage.shape[1])]

    def copy(k):
        src, _, stage, r = blocks[k]
        return pltpu.make_async_copy(src.at[pl.ds(r, stage.shape[1])],
                                     stage.at[k % N_STAGE], sem_ref.at[k % N_STAGE])

    for k in range(min(N_STAGE - 1, len(blocks))):
        copy(k).start()
    for k, (_, dst, stage, r) in enumerate(blocks):
        if k + N_STAGE - 1 < len(blocks):
            copy(k + N_STAGE - 1).start()
        copy(k).wait()
        dst[r:r + stage.shape[1], :] = stage[k % N_STAGE].astype(jnp.bfloat16)


def _first_grid_step():
    return (pl.program_id(0) == 0) & (pl.program_id(1) == 0)


def _mixer_kernel(x_ref, g_ref, w_in_hbm, conv_w_ref, ln_g_ref, ln_b_ref,
                  ws_ref, bst_ref, w_out_hbm, o_ref,
                  w_in_ref, w_out_ref, stage_in_ref, stage_out_ref, sem_ref,
                  h_ref, proj_ref, carry_ref, mix_ref):
    tm, d = x_ref.shape
    n_col_chunks = d // CW
    n_row_blocks = tm // CHUNK
    heads_per_chunk = CW // HEAD_DIM

    @pl.when(_first_grid_step())
    def _():
        _load_weights_bf16([(w_in_hbm, w_in_ref, stage_in_ref),
                            (w_out_hbm, w_out_ref, stage_out_ref)], sem_ref)

    @pl.when(pl.program_id(1) == 0)
    def _():
        carry_ref[...] = jnp.zeros_like(carry_ref)

    h_ref[...] = _rms(x_ref[...], g_ref[...]).astype(jnp.bfloat16)

    tri = (lax.broadcasted_iota(jnp.int32, (CHUNK, CHUNK), 0)
           >= lax.broadcasted_iota(jnp.int32, (CHUNK, CHUNK), 1))

    def in_proj(j):
        for s in range(N_SECT):
            proj_ref[j % 2, :, s * CW:(s + 1) * CW] = jnp.dot(
                h_ref[...], w_in_ref[:, s * d + j * CW:s * d + (j + 1) * CW],
                preferred_element_type=jnp.float32)

    def out_proj(c0, c1):
        part = [jnp.dot(mix_ref[:, o + c0:o + c1], w_out_ref[o + c0:o + c1, :],
                        preferred_element_type=jnp.float32) for o in (0, d)]
        return part[0] + part[1]

    in_proj(0)
    for j in range(n_col_chunks):
        cols = slice(j * CW, (j + 1) * CW)
        if j + 1 < n_col_chunks:
            in_proj(j + 1)
        else:
            o_ref[...] = x_ref[...] + out_proj(0, j * CW)

        def sect(s, rows, j=j):
            return proj_ref[j % 2, rows, s * CW:(s + 1) * CW]

        w0 = conv_w_ref[0:1, cols]
        w1 = conv_w_ref[1:2, cols]
        w2 = conv_w_ref[2:3, cols]
        wc = [jnp.where(tri, ws_ref[j * heads_per_chunk + hh], 0.0).astype(jnp.bfloat16)
              for hh in range(heads_per_chunk)]

        for r in range(n_row_blocks):
            rows = slice(r * CHUNK, (r + 1) * CHUNK)
            g = sect(1, rows) * sect(2, rows)
            if r == 0:
                prev = carry_ref[:, cols]
            else:
                prows = slice(r * CHUNK - SUBLANES, r * CHUNK)
                prev = sect(1, prows) * sect(2, prows)
            if r == n_row_blocks - 1:
                carry_ref[:, cols] = g[CHUNK - SUBLANES:, :]
            ext = jnp.concatenate([prev, g], axis=0)
            g1 = ext[SUBLANES - 1:SUBLANES - 1 + CHUNK, :]
            g2 = ext[SUBLANES - 2:SUBLANES - 2 + CHUNK, :]
            conv = w0 * g2 + w1 * g1 + w2 * g
            za = sect(3, rows)
            a = sect(0, rows) * conv * (za * _sigmoid(za))
            mix_ref[rows, cols] = a.astype(jnp.bfloat16)

            u = _gelu_tanh(sect(4, rows))
            v = _gelu_tanh(sect(5, rows))
            zb = sect(6, rows)
            sps = []
            for hh in range(heads_per_chunk):
                hc = slice(hh * HEAD_DIM, (hh + 1) * HEAD_DIM)
                gcols = slice(j * CW + hh * HEAD_DIM, j * CW + (hh + 1) * HEAD_DIM)
                vh = v[:, hc]
                mu = jnp.mean(vh, axis=-1, keepdims=True)
                vc = vh - mu
                var = jnp.mean(vc * vc, axis=-1, keepdims=True)
                vn = vc * lax.rsqrt(var + EPS) * ln_g_ref[:, gcols] + ln_b_ref[:, gcols]
                sp = jnp.dot(wc[hh], vn.astype(jnp.bfloat16),
                             preferred_element_type=jnp.float32)
                head = j * heads_per_chunk + hh
                sps.append(sp + bst_ref[:, head:head + 1])
            sp = jnp.concatenate(sps, axis=-1)
            bo = u * sp * (zb * _sigmoid(zb))
            mix_ref[rows, d + j * CW:d + (j + 1) * CW] = bo.astype(jnp.bfloat16)

    o_ref[...] += out_proj((n_col_chunks - 1) * CW, d)


def _attn_kernel(x_ref, mem_ref, gx_ref, gm_ref, w_q_hbm, w_kv_hbm, w_xo_hbm, gf_ref,
                 o_ref,
                 w_q_ref, w_kv_ref, w_xo_ref, stage_d_ref, stage_kv_ref, sem_ref,
                 kt_ref, v_ref, att_ref):
    tm, d = x_ref.shape
    hd = d // X_HEADS

    @pl.when(_first_grid_step())
    def _():
        _load_weights_bf16([(w_q_hbm, w_q_ref, stage_d_ref),
                            (w_kv_hbm, w_kv_ref, stage_kv_ref),
                            (w_xo_hbm, w_xo_ref, stage_d_ref)], sem_ref)

    @pl.when(pl.program_id(1) == 0)
    def _():
        m = _rms(mem_ref[...], gm_ref[...]).astype(jnp.bfloat16)
        kv = jnp.dot(m, w_kv_ref[...], preferred_element_type=jnp.float32)
        kt_ref[...] = kv[:, :d].T.astype(jnp.bfloat16)
        v_ref[...] = kv[:, d:].astype(jnp.bfloat16)

    halves = [slice(i * ATTN_HALF, (i + 1) * ATTN_HALF) for i in range(tm // ATTN_HALF)]
    hs = [_rms(x_ref[r, :], gx_ref[...]).astype(jnp.bfloat16) for r in halves]
    qs = [(jnp.dot(h, w_q_ref[...], preferred_element_type=jnp.float32)
           * (1.0 / math.sqrt(hd))).astype(jnp.bfloat16) for h in hs]
    for hh in range(X_HEADS):
        hc = slice(hh * hd, (hh + 1) * hd)
        for q, r in zip(qs, halves):
            s = jnp.dot(q[:, hc], kt_ref[hc, :], preferred_element_type=jnp.float32)
            p = jnp.exp(s - jnp.max(s, axis=-1, keepdims=True))
            l = jnp.sum(p, axis=-1, keepdims=True)
            o = jnp.dot(p.astype(jnp.bfloat16), v_ref[:, hc],
                        preferred_element_type=jnp.float32)
            att_ref[r, hc] = (o / l).astype(jnp.bfloat16)
    for r in halves:
        y = x_ref[r, :] + jnp.dot(att_ref[r, :], w_xo_ref[...],
                                  preferred_element_type=jnp.float32)
        o_ref[r, :] = _rms(y, gf_ref[...])


def _stage(cols):
    rows = 1 << int(math.log2(STAGE_BYTES // (4 * cols)))
    return pltpu.VMEM((N_STAGE, rows, cols), jnp.float32)


def _resident(shape):
    return pl.BlockSpec(shape, lambda *_: (0,) * len(shape),
                        pipeline_mode=pl.Buffered(1))


def kernel(x, mem, norm_mix_g, w_in, conv_w, gm_ln_g, gm_ln_b, gm_ws, gm_bs, w_out,
           norm_x_g, norm_mem_g, w_q, w_kv, w_xo, norm_final_g):
    b, s, d = x.shape
    m_len = mem.shape[1]
    assert w_in.shape[0] == 1, "the final norm is fused into the (single) layer's attention call"
    assert s % TM == 0 and TM % CHUNK == 0 and d % CW == 0
    assert s % TM_ATTN == 0 and TM_ATTN % ATTN_HALF == 0
    bf16 = jnp.bfloat16
    f32 = jnp.float32
    params = pltpu.CompilerParams(
        dimension_semantics=("arbitrary", "arbitrary"), vmem_limit_bytes=VMEM_LIMIT)
    row = lambda a: a.reshape(1, -1)
    hbm = pl.BlockSpec(memory_space=pl.ANY)
    tile = pl.BlockSpec((None, TM, d), lambda i, t: (i, t, 0))
    sems = pltpu.SemaphoreType.DMA((N_STAGE,))

    x = pl.pallas_call(
        _mixer_kernel,
        grid=(b, s // TM),
        in_specs=[tile,
                  _resident((1, d)),
                  hbm,
                  _resident((CONV_K, d)),
                  _resident((1, d)), _resident((1, d)),
                  _resident(gm_ws.shape[1:]),
                  _resident((CHUNK, gm_bs.shape[1])),
                  hbm],
        out_specs=tile,
        out_shape=jax.ShapeDtypeStruct((b, s, d), f32),
        scratch_shapes=[pltpu.VMEM((d, N_SECT * d), bf16),
                        pltpu.VMEM((2 * d, d), bf16),
                        _stage(N_SECT * d), _stage(d), sems,
                        pltpu.VMEM((TM, d), bf16),
                        pltpu.VMEM((2, TM, N_SECT * CW), f32),
                        pltpu.VMEM((SUBLANES, d), f32),
                        pltpu.VMEM((TM, 2 * d), bf16)],
        compiler_params=params,
        name="mixer",
    )(x, row(norm_mix_g[0]), w_in[0], conv_w[0], row(gm_ln_g[0]), row(gm_ln_b[0]),
      gm_ws[0], gm_bs[0].T, w_out[0])

    attn_tile = pl.BlockSpec((None, TM_ATTN, d), lambda i, t: (i, t, 0))
    return pl.pallas_call(
        _attn_kernel,
        grid=(b, s // TM_ATTN),
        in_specs=[attn_tile,
                  pl.BlockSpec((None, m_len, d), lambda i, t: (i, 0, 0)),
                  _resident((1, d)), _resident((1, d)),
                  hbm, hbm, hbm,
                  _resident((1, d))],
        out_specs=attn_tile,
        out_shape=jax.ShapeDtypeStruct((b, s, d), f32),
        scratch_shapes=[pltpu.VMEM((d, d), bf16),
                        pltpu.VMEM((d, 2 * d), bf16),
                        pltpu.VMEM((d, d), bf16),
                        _stage(d), _stage(2 * d), sems,
                        pltpu.VMEM((d, m_len), bf16),
                        pltpu.VMEM((m_len, d), bf16),
                        pltpu.VMEM((TM_ATTN, d), bf16)],
        compiler_params=params,
        name="xattn",
    )(x, mem, row(norm_x_g[0]), row(norm_mem_g[0]), w_q[0], w_kv[0], w_xo[0],
      row(norm_final_g))
```

```python
import math

import jax
import jax.numpy as jnp
from jax import lax
from jax.experimental import pallas as pl
from jax.experimental.pallas import tpu as pltpu

EPS = 1e-6
CONV_K = 3
CHUNK = 128
HEAD_DIM = 128
N_SECT = 7
X_HEADS = 4
SUBLANES = 8

TM = 512
TM_ATTN = 1024
ATTN_HALF = 512
CW = 256
N_STAGE = 4
STAGE_BYTES = 1 << 20
VMEM_LIMIT = 56 * 1024 * 1024


def _rms(x, g):
    ms = jnp.mean(x * x, axis=-1, keepdims=True)
    return x * lax.rsqrt(ms + EPS) * g


def _sigmoid(z):
    return 0.5 * (1.0 + jnp.tanh(0.5 * z))


def _gelu_tanh(x):
    c = math.sqrt(2.0 / math.pi)
    return 0.5 * x * (1.0 + jnp.tanh(c * (x + 0.044715 * (x * x * x))))


def _load_weights_bf16(jobs, sem_ref):
    blocks = [(src, dst, stage, r)
              for src, dst, stage in jobs
              for r in range(0, src.shape[0], stage.shape[1])]

    def copy(k):
        src, _, stage, r = blocks[k]
        return pltpu.make_async_copy(src.at[pl.ds(r, stage.shape[1])],
                                     stage.at[k % N_STAGE], sem_ref.at[k % N_STAGE])

    for k in range(min(N_STAGE - 1, len(blocks))):
        copy(k).start()
    for k, (_, dst, stage, r) in enumerate(blocks):
        if k + N_STAGE - 1 < len(blocks):
            copy(k + N_STAGE - 1).start()
        copy(k).wait()
        dst[r:r + stage.shape[1], :] = stage[k % N_STAGE].astype(jnp.bfloat16)


def _first_grid_step():
    return (pl.program_id(0) == 0) & (pl.program_id(1) == 0)


def _mixer_kernel(x_ref, g_ref, w_in_hbm, conv_w_ref, ln_g_ref, ln_b_ref,
                  ws_ref, bst_ref, w_out_hbm, o_ref,
                  w_in_ref, w_out_ref, stage_in_ref, stage_out_ref, sem_ref,
                  h_ref, proj_ref, carry_ref, mix_ref):
    tm, d = x_ref.shape
    n_col_chunks = d // CW
    n_row_blocks = tm // CHUNK
    heads_per_chunk = CW // HEAD_DIM

    @pl.when(_first_grid_step())
    def _():
        _load_weights_bf16([(w_in_hbm, w_in_ref, stage_in_ref),
                            (w_out_hbm, w_out_ref, stage_out_ref)], sem_ref)

    @pl.when(pl.program_id(1) == 0)
    def _():
        carry_ref[...] = jnp.zeros_like(carry_ref)

    h_ref[...] = _rms(x_ref[...], g_ref[...]).astype(jnp.bfloat16)

    tri = (lax.broadcasted_iota(jnp.int32, (CHUNK, CHUNK), 0)
           >= lax.broadcasted_iota(jnp.int32, (CHUNK, CHUNK), 1))

    def in_proj(j):
        for s in range(N_SECT):
            proj_ref[j % 2, :, s * CW:(s + 1) * CW] = jnp.dot(
                h_ref[...], w_in_ref[:, s * d + j * CW:s * d + (j + 1) * CW],
                preferred_element_type=jnp.float32)

    def out_proj(c0, c1):
        part = [jnp.dot(mix_ref[:, o + c0:o + c1], w_out_ref[o + c0:o + c1, :],
                        preferred_element_type=jnp.float32) for o in (0, d)]
        return part[0] + part[1]

    in_proj(0)
    for j in range(n_col_chunks):
        cols = slice(j * CW, (j + 1) * CW)
        if j + 1 < n_col_chunks:
            in_proj(j + 1)
        else:
            o_ref[...] = x_ref[...] + out_proj(0, j * CW)

        def sect(s, rows, j=j):
            return proj_ref[j % 2, rows, s * CW:(s + 1) * CW]

        w0 = conv_w_ref[0:1, cols]
        w1 = conv_w_ref[1:2, cols]
        w2 = conv_w_ref[2:3, cols]
        wc = [jnp.where(tri, ws_ref[j * heads_per_chunk + hh], 0.0).astype(jnp.bfloat16)
              for hh in range(heads_per_chunk)]

        for r in range(n_row_blocks):
            rows = slice(r * CHUNK, (r + 1) * CHUNK)
            g = sect(1, rows) * sect(2, rows)
            if r == 0:
                prev = carry_ref[:, cols]
            else:
                prows = slice(r * CHUNK - SUBLANES, r * CHUNK)
                prev = sect(1, prows) * sect(2, prows)
            if r == n_row_blocks - 1:
                carry_ref[:, cols] = g[CHUNK - SUBLANES:, :]
            ext = jnp.concatenate([prev, g], axis=0)
            g1 = ext[SUBLANES - 1:SUBLANES - 1 + CHUNK, :]
            g2 = ext[SUBLANES - 2:SUBLANES - 2 + CHUNK, :]
            conv = w0 * g2 + w1 * g1 + w2 * g
            za = sect(3, rows)
            a = sect(0, rows) * conv * (za * _sigmoid(za))
            mix_ref[rows, cols] = a.astype(jnp.bfloat16)

            u = _gelu_tanh(sect(4, rows))
            v = _gelu_tanh(sect(5, rows))
            zb = sect(6, rows)
            sps = []
            for hh in range(heads_per_chunk):
                hc = slice(hh * HEAD_DIM, (hh + 1) * HEAD_DIM)
                gcols = slice(j * CW + hh * HEAD_DIM, j * CW + (hh + 1) * HEAD_DIM)
                vh = v[:, hc]
                mu = jnp.mean(vh, axis=-1, keepdims=True)
                vc = vh - mu
                var = jnp.mean(vc * vc, axis=-1, keepdims=True)
                vn = vc * lax.rsqrt(var + EPS) * ln_g_ref[:, gcols] + ln_b_ref[:, gcols]
                sp = jnp.dot(wc[hh], vn.astype(jnp.bfloat16),
                             preferred_element_type=jnp.float32)
                head = j * heads_per_chunk + hh
                sps.append(sp + bst_ref[:, head:head + 1])
            sp = jnp.concatenate(sps, axis=-1)
            bo = u * sp * (zb * _sigmoid(zb))
            mix_ref[rows, d + j * CW:d + (j + 1) * CW] = bo.astype(jnp.bfloat16)

    o_ref[...] += out_proj((n_col_chunks - 1) * CW, d)


def _attn_kernel(x_ref, mem_ref, gx_ref, gm_ref, w_q_hbm, w_kv_hbm, w_xo_hbm, gf_ref,
                 o_ref,
                 w_q_ref, w_kv_ref, w_xo_ref, stage_d_ref, stage_kv_ref, sem_ref,
                 wqk_ref, vxo_ref, p_ref):
    tm, d = x_ref.shape
    m_len = mem_ref.shape[0]
    hd = d // X_HEADS
    bf16 = jnp.bfloat16

    @pl.when(_first_grid_step())
    def _():
        _load_weights_bf16([(w_q_hbm, w_q_ref, stage_d_ref),
                            (w_kv_hbm, w_kv_ref, stage_kv_ref),
                            (w_xo_hbm, w_xo_ref, stage_d_ref)], sem_ref)

    @pl.when(pl.program_id(1) == 0)
    def _():
        m = _rms(mem_ref[...], gm_ref[...]).astype(bf16)
        kv = jnp.dot(m, w_kv_ref[...], preferred_element_type=jnp.float32)
        for hh in range(X_HEADS):
            hc = slice(hh * hd, (hh + 1) * hd)
            mc = slice(hh * m_len, (hh + 1) * m_len)
            k_h = kv[:, hc].astype(bf16)
            v_h = kv[:, d + hh * hd:d + (hh + 1) * hd].astype(bf16)
            wqk = lax.dot_general(w_q_ref[:, hc], k_h, (((1,), (1,)), ((), ())),
                                  preferred_element_type=jnp.float32)
            wqk_ref[:, mc] = (wqk * (1.0 / math.sqrt(hd))).astype(bf16)
            vxo_ref[mc, :] = jnp.dot(v_h, w_xo_ref[hc, :],
                                     preferred_element_type=jnp.float32).astype(bf16)

    halves = [slice(i * ATTN_HALF, (i + 1) * ATTN_HALF) for i in range(tm // ATTN_HALF)]
    hs = [_rms(x_ref[r, :], gx_ref[...]).astype(bf16) for r in halves]
    scores = [jnp.dot(h, wqk_ref[...], preferred_element_type=jnp.float32) for h in hs]
    for hh in range(X_HEADS):
        mc = slice(hh * m_len, (hh + 1) * m_len)
        for s, r in zip(scores, halves):
            p = jnp.exp(s[:, mc] - jnp.max(s[:, mc], axis=-1, keepdims=True))
            l = jnp.sum(p, axis=-1, keepdims=True)
            p_ref[r, mc] = (p * (1.0 / l)).astype(bf16)
    for r in halves:
        y = x_ref[r, :] + jnp.dot(p_ref[r, :], vxo_ref[...],
                                  preferred_element_type=jnp.float32)
        o_ref[r, :] = _rms(y, gf_ref[...])


def _stage(cols):
    rows = 1 << int(math.log2(STAGE_BYTES // (4 * cols)))
    return pltpu.VMEM((N_STAGE, rows, cols), jnp.float32)


def _resident(shape):
    return pl.BlockSpec(shape, lambda *_: (0,) * len(shape),
                        pipeline_mode=pl.Buffered(1))


def kernel(x, mem, norm_mix_g, w_in, conv_w, gm_ln_g, gm_ln_b, gm_ws, gm_bs, w_out,
           norm_x_g, norm_mem_g, w_q, w_kv, w_xo, norm_final_g):
    b, s, d = x.shape
    m_len = mem.shape[1]
    assert w_in.shape[0] == 1, "the final norm is fused into the (single) layer's attention call"
    assert s % TM == 0 and TM % CHUNK == 0 and d % CW == 0
    assert s % TM_ATTN == 0 and TM_ATTN % ATTN_HALF == 0
    bf16 = jnp.bfloat16
    f32 = jnp.float32
    params = pltpu.CompilerParams(
        dimension_semantics=("arbitrary", "arbitrary"), vmem_limit_bytes=VMEM_LIMIT)
    row = lambda a: a.reshape(1, -1)
    hbm = pl.BlockSpec(memory_space=pl.ANY)
    tile = pl.BlockSpec((None, TM, d), lambda i, t: (i, t, 0))
    sems = pltpu.SemaphoreType.DMA((N_STAGE,))

    x = pl.pallas_call(
        _mixer_kernel,
        grid=(b, s // TM),
        in_specs=[tile,
                  _resident((1, d)),
                  hbm,
                  _resident((CONV_K, d)),
                  _resident((1, d)), _resident((1, d)),
                  _resident(gm_ws.shape[1:]),
                  _resident((CHUNK, gm_bs.shape[1])),
                  hbm],
        out_specs=tile,
        out_shape=jax.ShapeDtypeStruct((b, s, d), f32),
        scratch_shapes=[pltpu.VMEM((d, N_SECT * d), bf16),
                        pltpu.VMEM((2 * d, d), bf16),
                        _stage(N_SECT * d), _stage(d), sems,
                        pltpu.VMEM((TM, d), bf16),
                        pltpu.VMEM((2, TM, N_SECT * CW), f32),
                        pltpu.VMEM((SUBLANES, d), f32),
                        pltpu.VMEM((TM, 2 * d), bf16)],
        compiler_params=params,
        name="mixer",
    )(x, row(norm_mix_g[0]), w_in[0], conv_w[0], row(gm_ln_g[0]), row(gm_ln_b[0]),
      gm_ws[0], gm_bs[0].T, w_out[0])

    attn_tile = pl.BlockSpec((None, TM_ATTN, d), lambda i, t: (i, t, 0))
    return pl.pallas_call(
        _attn_kernel,
        grid=(b, s // TM_ATTN),
        in_specs=[attn_tile,
                  pl.BlockSpec((None, m_len, d), lambda i, t: (i, 0, 0)),
                  _resident((1, d)), _resident((1, d)),
                  hbm, hbm, hbm,
                  _resident((1, d))],
        out_specs=attn_tile,
        out_shape=jax.ShapeDtypeStruct((b, s, d), f32),
        scratch_shapes=[pltpu.VMEM((d, d), bf16),
                        pltpu.VMEM((d, 2 * d), bf16),
                        pltpu.VMEM((d, d), bf16),
                        _stage(d), _stage(2 * d), sems,
                        pltpu.VMEM((d, X_HEADS * m_len), bf16),
                        pltpu.VMEM((X_HEADS * m_len, d), bf16),
                        pltpu.VMEM((TM_ATTN, X_HEADS * m_len), bf16)],
        compiler_params=params,
        name="xattn",
    )(x, mem, row(norm_x_g[0]), row(norm_mem_g[0]), w_q[0], w_kv[0], w_xo[0],
      row(norm_final_g))
```

```python
import math

import jax
import jax.numpy as jnp
from jax import lax
from jax.experimental import pallas as pl
from jax.experimental.pallas import tpu as pltpu

EPS = 1e-6
CONV_K = 3
CHUNK = 128
HEAD_DIM = 128
N_SECT = 7
X_HEADS = 4
SUBLANES = 8

TM = 1024
MIX_SUB = 512
ATTN_HALF = 512
CW = 256
N_STAGE = 4
STAGE_BYTES = 1 << 19
VMEM_LIMIT = 56 * 1024 * 1024


def _rms(x, g):
    ms = jnp.mean(x * x, axis=-1, keepdims=True)
    return x * lax.rsqrt(ms + EPS) * g


def _sigmoid(z):
    return 0.5 * (1.0 + jnp.tanh(0.5 * z))


def _gelu_tanh(x):
    c = math.sqrt(2.0 / math.pi)
    return 0.5 * x * (1.0 + jnp.tanh(c * (x + 0.044715 * (x * x * x))))


def _load_weights_bf16(jobs, sem_ref):
    blocks = [(src, dst, stage, r)
              for src, dst, stage in jobs
              for r in range(0, src.shape[0], stage.shape[1])]

    def copy(k):
        src, _, stage, r = blocks[k]
        return pltpu.make_async_copy(src.at[pl.ds(r, stage.shape[1])],
                                     stage.at[k % N_STAGE], sem_ref.at[k % N_STAGE])

    for k in range(min(N_STAGE - 1, len(blocks))):
        copy(k).start()
    for k, (_, dst, stage, r) in enumerate(blocks):
        if k + N_STAGE - 1 < len(blocks):
            copy(k + N_STAGE - 1).start()
        copy(k).wait()
        dst[r:r + stage.shape[1], :] = stage[k % N_STAGE].astype(jnp.bfloat16)


def _first_grid_step():
    return (pl.program_id(0) == 0) & (pl.program_id(1) == 0)


def _mixer_kernel(x_ref, g_ref, w_in_hbm, conv_w_ref, ln_g_ref, ln_b_ref,
                  ws_ref, bs_ref, w_out_hbm, o_ref,
                  w_in_ref, w_out_ref, stage_in_ref, stage_out_ref, sem_ref,
                  h_ref, proj_ref, carry_ref, mix_ref):
    tm, d = x_ref.shape
    n_sub = tm // MIX_SUB
    n_col_chunks = d // CW
    n_row_blocks = MIX_SUB // CHUNK
    heads_per_chunk = CW // HEAD_DIM
    bf16 = jnp.bfloat16

    @pl.when(_first_grid_step())
    def _():
        _load_weights_bf16([(w_in_hbm, w_in_ref, stage_in_ref),
                            (w_out_hbm, w_out_ref, stage_out_ref)], sem_ref)

    @pl.when(pl.program_id(1) == 0)
    def _():
        carry_ref[...] = jnp.zeros_like(carry_ref)

    tri = (lax.broadcasted_iota(jnp.int32, (CHUNK, CHUNK), 0)
           >= lax.broadcasted_iota(jnp.int32, (CHUNK, CHUNK), 1))
    bst = bs_ref[...].T

    def sub_rows(k):
        return slice(k * MIX_SUB, (k + 1) * MIX_SUB)

    def norm(k):
        h_ref[k] = _rms(x_ref[sub_rows(k), :], g_ref[...]).astype(bf16)

    def in_proj(k, j):
        for s in range(N_SECT):
            proj_ref[j % 2, :, s * CW:(s + 1) * CW] = jnp.dot(
                h_ref[k], w_in_ref[:, s * d + j * CW:s * d + (j + 1) * CW],
                preferred_element_type=jnp.float32)

    def out_proj(k, c0, c1):
        part = [jnp.dot(mix_ref[k, :, o + c0:o + c1], w_out_ref[o + c0:o + c1, :],
                        preferred_element_type=jnp.float32) for o in (0, d)]
        return part[0] + part[1]

    def vector_work(k, j):
        cols = slice(j * CW, (j + 1) * CW)

        def sect(s, rows):
            return proj_ref[j % 2, rows, s * CW:(s + 1) * CW]

        w0 = conv_w_ref[0:1, cols]
        w1 = conv_w_ref[1:2, cols]
        w2 = conv_w_ref[2:3, cols]
        wc = [jnp.where(tri, ws_ref[j * heads_per_chunk + hh], 0.0).astype(bf16)
              for hh in range(heads_per_chunk)]

        for r in range(n_row_blocks):
            rows = slice(r * CHUNK, (r + 1) * CHUNK)
            g = sect(1, rows) * sect(2, rows)
            if r == 0:
                prev = carry_ref[:, cols]
            else:
                prows = slice(r * CHUNK - SUBLANES, r * CHUNK)
                prev = sect(1, prows) * sect(2, prows)
            if r == n_row_blocks - 1:
                carry_ref[:, cols] = g[CHUNK - SUBLANES:, :]
            ext = jnp.concatenate([prev, g], axis=0)
            g1 = ext[SUBLANES - 1:SUBLANES - 1 + CHUNK, :]
            g2 = ext[SUBLANES - 2:SUBLANES - 2 + CHUNK, :]
            conv = w0 * g2 + w1 * g1 + w2 * g
            za = sect(3, rows)
            a = sect(0, rows) * conv * (za * _sigmoid(za))
            mix_ref[k, rows, cols] = a.astype(bf16)

            u = _gelu_tanh(sect(4, rows))
            v = _gelu_tanh(sect(5, rows))
            zb = sect(6, rows)
            sps = []
            for hh in range(heads_per_chunk):
                hc = slice(hh * HEAD_DIM, (hh + 1) * HEAD_DIM)
                gcols = slice(j * CW + hh * HEAD_DIM, j * CW + (hh + 1) * HEAD_DIM)
                vh = v[:, hc]
                mu = jnp.mean(vh, axis=-1, keepdims=True)
                vc = vh - mu
                var = jnp.mean(vc * vc, axis=-1, keepdims=True)
                vn = vc * lax.rsqrt(var + EPS) * ln_g_ref[:, gcols] + ln_b_ref[:, gcols]
                sp = jnp.dot(wc[hh], vn.astype(bf16), preferred_element_type=jnp.float32)
                head = j * heads_per_chunk + hh
                sps.append(sp + bst[:, head:head + 1])
            sp = jnp.concatenate(sps, axis=-1)
            bo = u * sp * (zb * _sigmoid(zb))
            mix_ref[k, rows, d + j * CW:d + (j + 1) * CW] = bo.astype(bf16)

    norm(0)
    in_proj(0, 0)
    for k in range(n_sub):
        for j in range(n_col_chunks):
            if j + 1 < n_col_chunks:
                in_proj(k, j + 1)
            else:
                o_ref[sub_rows(k), :] = x_ref[sub_rows(k), :] + out_proj(k, 0, j * CW)
                if k + 1 < n_sub:
                    norm(k + 1)
                    in_proj(k + 1, 0)
            vector_work(k, j)
        o_ref[sub_rows(k), :] += out_proj(k, (n_col_chunks - 1) * CW, d)


def _attn_kernel(x_ref, mem_ref, gx_ref, gm_ref, w_q_hbm, w_kv_hbm, w_xo_hbm, gf_ref,
                 o_ref,
                 w_q_ref, w_kv_ref, w_xo_ref, stage_d_ref, stage_kv_ref, sem_ref,
                 wqk_ref, vxo_ref, p_ref):
    tm, d = x_ref.shape
    m_len = mem_ref.shape[0]
    hd = d // X_HEADS
    bf16 = jnp.bfloat16

    @pl.when(_first_grid_step())
    def _():
        _load_weights_bf16([(w_q_hbm, w_q_ref, stage_d_ref),
                            (w_kv_hbm, w_kv_ref, stage_kv_ref),
                            (w_xo_hbm, w_xo_ref, stage_d_ref)], sem_ref)

    @pl.when(pl.program_id(1) == 0)
    def _():
        m = _rms(mem_ref[...], gm_ref[...]).astype(bf16)
        kv = jnp.dot(m, w_kv_ref[...], preferred_element_type=jnp.float32)
        for hh in range(X_HEADS):
            hc = slice(hh * hd, (hh + 1) * hd)
            mc = slice(hh * m_len, (hh + 1) * m_len)
            k_h = kv[:, hc].astype(bf16)
            v_h = kv[:, d + hh * hd:d + (hh + 1) * hd].astype(bf16)
            wqk = lax.dot_general(w_q_ref[:, hc], k_h, (((1,), (1,)), ((), ())),
                                  preferred_element_type=jnp.float32)
            wqk_ref[:, mc] = (wqk * (1.0 / math.sqrt(hd))).astype(bf16)
            vxo_ref[mc, :] = jnp.dot(v_h, w_xo_ref[hc, :],
                                     preferred_element_type=jnp.float32).astype(bf16)

    halves = [slice(i * ATTN_HALF, (i + 1) * ATTN_HALF) for i in range(tm // ATTN_HALF)]
    hs = [_rms(x_ref[r, :], gx_ref[...]).astype(bf16) for r in halves]
    scores = [jnp.dot(h, wqk_ref[...], preferred_element_type=jnp.float32) for h in hs]
    for hh in range(X_HEADS):
        mc = slice(hh * m_len, (hh + 1) * m_len)
        for s, r in zip(scores, halves):
            p = jnp.exp(s[:, mc] - jnp.max(s[:, mc], axis=-1, keepdims=True))
            l = jnp.sum(p, axis=-1, keepdims=True)
            p_ref[r, mc] = (p * (1.0 / l)).astype(bf16)
    for r in halves:
        y = x_ref[r, :] + jnp.dot(p_ref[r, :], vxo_ref[...],
                                  preferred_element_type=jnp.float32)
        o_ref[r, :] = _rms(y, gf_ref[...])


def _stage(cols):
    rows = 1 << int(math.log2(STAGE_BYTES // (4 * cols)))
    return pltpu.VMEM((N_STAGE, rows, cols), jnp.float32)


def _resident(shape):
    return pl.BlockSpec((None,) + tuple(shape[1:]), lambda *_: (0,) * len(shape),
                        pipeline_mode=pl.Buffered(1))


def kernel(x, mem, norm_mix_g, w_in, conv_w, gm_ln_g, gm_ln_b, gm_ws, gm_bs, w_out,
           norm_x_g, norm_mem_g, w_q, w_kv, w_xo, norm_final_g):
    b, s, d = x.shape
    m_len = mem.shape[1]
    assert w_in.shape[0] == 1, "the final norm is fused into the (single) layer's attention call"
    assert s % TM == 0 and TM % MIX_SUB == 0 and MIX_SUB % CHUNK == 0 and d % CW == 0
    assert TM % ATTN_HALF == 0
    bf16 = jnp.bfloat16
    f32 = jnp.float32
    params = pltpu.CompilerParams(
        dimension_semantics=("arbitrary", "arbitrary"), vmem_limit_bytes=VMEM_LIMIT)
    gain = lambda g: g.reshape(1, 1, d)
    hbm = pl.BlockSpec(memory_space=pl.ANY)
    tile = pl.BlockSpec((None, TM, d), lambda i, t: (i, t, 0))
    sems = pltpu.SemaphoreType.DMA((N_STAGE,))
    n_sub = TM // MIX_SUB

    x = pl.pallas_call(
        _mixer_kernel,
        grid=(b, s // TM),
        in_specs=[tile,
                  _resident((1, 1, d)),
                  hbm,
                  _resident(conv_w.shape),
                  _resident((1, 1, d)), _resident((1, 1, d)),
                  _resident(gm_ws.shape),
                  _resident(gm_bs.shape),
                  hbm],
        out_specs=tile,
        out_shape=jax.ShapeDtypeStruct((b, s, d), f32),
        scratch_shapes=[pltpu.VMEM((d, N_SECT * d), bf16),
                        pltpu.VMEM((2 * d, d), bf16),
                        _stage(N_SECT * d), _stage(d), sems,
                        pltpu.VMEM((n_sub, MIX_SUB, d), bf16),
                        pltpu.VMEM((2, MIX_SUB, N_SECT * CW), f32),
                        pltpu.VMEM((SUBLANES, d), f32),
                        pltpu.VMEM((n_sub, MIX_SUB, 2 * d), bf16)],
        compiler_params=params,
        name="mixer",
    )(x, gain(norm_mix_g), w_in[0], conv_w, gain(gm_ln_g), gain(gm_ln_b), gm_ws, gm_bs,
      w_out[0])

    return pl.pallas_call(
        _attn_kernel,
        grid=(b, s // TM),
        in_specs=[tile,
                  pl.BlockSpec((None, m_len, d), lambda i, t: (i, 0, 0)),
                  _resident((1, 1, d)), _resident((1, 1, d)),
                  hbm, hbm, hbm,
                  _resident((1, 1, d))],
        out_specs=tile,
        out_shape=jax.ShapeDtypeStruct((b, s, d), f32),
        scratch_shapes=[pltpu.VMEM((d, d), bf16),
                        pltpu.VMEM((d, 2 * d), bf16),
                        pltpu.VMEM((d, d), bf16),
                        _stage(d), _stage(2 * d), sems,
                        pltpu.VMEM((d, X_HEADS * m_len), bf16),
                        pltpu.VMEM((X_HEADS * m_len, d), bf16),
                        pltpu.VMEM((TM, X_HEADS * m_len), bf16)],
        compiler_params=params,
        name="xattn",
    )(x, mem, gain(norm_x_g), gain(norm_mem_g), w_q[0], w_kv[0], w_xo[0],
      gain(norm_final_g))
```

```python
import math

import jax
import jax.numpy as jnp
from jax import lax
from jax.experimental import pallas as pl
from jax.experimental.pallas import tpu as pltpu

EPS = 1e-6
CONV_K = 3
CHUNK = 128
HEAD_DIM = 128
N_SECT = 7
X_HEADS = 4
SUBLANES = 8

TM_MIX = 512
MIX_SUB = 512
TM_ATTN = 1024
ATTN_HALF = 512
CW = 256
N_STAGE = 4
STAGE_BYTES = 1 << 20
VMEM_LIMIT = 56 * 1024 * 1024


def _rms(x, g):
    ms = jnp.mean(x * x, axis=-1, keepdims=True)
    return x * lax.rsqrt(ms + EPS) * g


def _sigmoid(z):
    return 0.5 * (1.0 + jnp.tanh(0.5 * z))


def _gelu_tanh(x):
    c = math.sqrt(2.0 / math.pi)
    return 0.5 * x * (1.0 + jnp.tanh(c * (x + 0.044715 * (x * x * x))))


def _load_weights_bf16(jobs, sem_ref):
    blocks = [(src, dst, stage, r)
              for src, dst, stage in jobs
              for r in range(0, src.shape[0], stage.shape[1])]

    def copy(k):
        src, _, stage, r = blocks[k]
        return pltpu.make_async_copy(src.at[pl.ds(r, stage.shape[1])],
                                     stage.at[k % N_STAGE], sem_ref.at[k % N_STAGE])

    for k in range(min(N_STAGE - 1, len(blocks))):
        copy(k).start(priority=k % 2)
    for k, (_, dst, stage, r) in enumerate(blocks):
        if k + N_STAGE - 1 < len(blocks):
            copy(k + N_STAGE - 1).start(priority=(k + N_STAGE - 1) % 2)
        copy(k).wait()
        dst[r:r + stage.shape[1], :] = stage[k % N_STAGE].astype(jnp.bfloat16)


def _first_grid_step():
    return (pl.program_id(0) == 0) & (pl.program_id(1) == 0)


def _mixer_kernel(x_ref, g_ref, w_in_hbm, conv_w_ref, ln_g_ref, ln_b_ref,
                  ws_ref, bs_ref, w_out_hbm, o_ref,
                  w_in_ref, w_out_ref, stage_in_ref, stage_out_ref, sem_ref,
                  h_ref, proj_ref, carry_ref, mix_ref):
    tm, d = x_ref.shape
    n_sub = tm // MIX_SUB
    n_col_chunks = d // CW
    n_row_blocks = MIX_SUB // CHUNK
    heads_per_chunk = CW // HEAD_DIM
    bf16 = jnp.bfloat16

    @pl.when(_first_grid_step())
    def _():
        _load_weights_bf16([(w_in_hbm, w_in_ref, stage_in_ref),
                            (w_out_hbm, w_out_ref, stage_out_ref)], sem_ref)

    @pl.when(pl.program_id(1) == 0)
    def _():
        carry_ref[...] = jnp.zeros_like(carry_ref)

    tri = (lax.broadcasted_iota(jnp.int32, (CHUNK, CHUNK), 0)
           >= lax.broadcasted_iota(jnp.int32, (CHUNK, CHUNK), 1))
    bst = bs_ref[...].T

    def sub_rows(k):
        return slice(k * MIX_SUB, (k + 1) * MIX_SUB)

    def norm(k):
        h_ref[k] = _rms(x_ref[sub_rows(k), :], g_ref[...]).astype(bf16)

    def in_proj(k, j):
        for s in range(N_SECT):
            proj_ref[j % 2, :, s * CW:(s + 1) * CW] = jnp.dot(
                h_ref[k], w_in_ref[:, s * d + j * CW:s * d + (j + 1) * CW],
                preferred_element_type=jnp.float32)

    def out_proj(k, c0, c1):
        part = [jnp.dot(mix_ref[k, :, o + c0:o + c1], w_out_ref[o + c0:o + c1, :],
                        preferred_element_type=jnp.float32) for o in (0, d)]
        return part[0] + part[1]

    def vector_work(k, j):
        cols = slice(j * CW, (j + 1) * CW)

        def sect(s, rows):
            return proj_ref[j % 2, rows, s * CW:(s + 1) * CW]

        w0 = conv_w_ref[0:1, cols]
        w1 = conv_w_ref[1:2, cols]
        w2 = conv_w_ref[2:3, cols]
        wc = [jnp.where(tri, ws_ref[j * heads_per_chunk + hh], 0.0).astype(bf16)
              for hh in range(heads_per_chunk)]

        for r in range(n_row_blocks):
            rows = slice(r * CHUNK, (r + 1) * CHUNK)
            g = sect(1, rows) * sect(2, rows)
            if r == 0:
                prev = carry_ref[:, cols]
            else:
                prows = slice(r * CHUNK - SUBLANES, r * CHUNK)
                prev = sect(1, prows) * sect(2, prows)
            if r == n_row_blocks - 1:
                carry_ref[:, cols] = g[CHUNK - SUBLANES:, :]
            ext = jnp.concatenate([prev, g], axis=0)
            g1 = ext[SUBLANES - 1:SUBLANES - 1 + CHUNK, :]
            g2 = ext[SUBLANES - 2:SUBLANES - 2 + CHUNK, :]
            conv = w0 * g2 + w1 * g1 + w2 * g
            za = sect(3, rows)
            a = sect(0, rows) * conv * (za * _sigmoid(za))
            mix_ref[k, rows, cols] = a.astype(bf16)

            u = _gelu_tanh(sect(4, rows))
            v = _gelu_tanh(sect(5, rows))
            zb = sect(6, rows)
            sps = []
            for hh in range(heads_per_chunk):
                hc = slice(hh * HEAD_DIM, (hh + 1) * HEAD_DIM)
                gcols = slice(j * CW + hh * HEAD_DIM, j * CW + (hh + 1) * HEAD_DIM)
                vh = v[:, hc]
                mu = jnp.mean(vh, axis=-1, keepdims=True)
                vc = vh - mu
                var = jnp.mean(vc * vc, axis=-1, keepdims=True)
                vn = vc * lax.rsqrt(var + EPS) * ln_g_ref[:, gcols] + ln_b_ref[:, gcols]
                sp = jnp.dot(wc[hh], vn.astype(bf16), preferred_element_type=jnp.float32)
                head = j * heads_per_chunk + hh
                sps.append(sp + bst[:, head:head + 1])
            sp = jnp.concatenate(sps, axis=-1)
            bo = u * sp * (zb * _sigmoid(zb))
            mix_ref[k, rows, d + j * CW:d + (j + 1) * CW] = bo.astype(bf16)

    norm(0)
    in_proj(0, 0)
    for k in range(n_sub):
        for j in range(n_col_chunks):
            if j + 1 < n_col_chunks:
                in_proj(k, j + 1)
            else:
                o_ref[sub_rows(k), :] = x_ref[sub_rows(k), :] + out_proj(k, 0, j * CW)
                if k + 1 < n_sub:
                    norm(k + 1)
                    in_proj(k + 1, 0)
            vector_work(k, j)
        o_ref[sub_rows(k), :] += out_proj(k, (n_col_chunks - 1) * CW, d)


def _attn_kernel(x_ref, mem_ref, gx_ref, gm_ref, w_q_hbm, w_kv_hbm, w_xo_hbm, gf_ref,
                 o_ref,
                 w_q_ref, w_kv_ref, w_xo_ref, stage_d_ref, stage_kv_ref, sem_ref,
                 wqk_ref, vxo_ref, p_ref):
    tm, d = x_ref.shape
    m_len = mem_ref.shape[0]
    hd = d // X_HEADS
    bf16 = jnp.bfloat16

    @pl.when(_first_grid_step())
    def _():
        _load_weights_bf16([(w_q_hbm, w_q_ref, stage_d_ref),
                            (w_kv_hbm, w_kv_ref, stage_kv_ref),
                            (w_xo_hbm, w_xo_ref, stage_d_ref)], sem_ref)

    @pl.when(pl.program_id(1) == 0)
    def _():
        m = _rms(mem_ref[...], gm_ref[...]).astype(bf16)
        kv = jnp.dot(m, w_kv_ref[...], preferred_element_type=jnp.float32)
        for hh in range(X_HEADS):
            hc = slice(hh * hd, (hh + 1) * hd)
            mc = slice(hh * m_len, (hh + 1) * m_len)
            k_h = kv[:, hc].astype(bf16)
            v_h = kv[:, d + hh * hd:d + (hh + 1) * hd].astype(bf16)
            wqk = lax.dot_general(w_q_ref[:, hc], k_h, (((1,), (1,)), ((), ())),
                                  preferred_element_type=jnp.float32)
            wqk_ref[:, mc] = (wqk * (1.0 / math.sqrt(hd))).astype(bf16)
            vxo_ref[mc, :] = jnp.dot(v_h, w_xo_ref[hc, :],
                                     preferred_element_type=jnp.float32).astype(bf16)

    halves = [slice(i * ATTN_HALF, (i + 1) * ATTN_HALF) for i in range(tm // ATTN_HALF)]
    hs = [_rms(x_ref[r, :], gx_ref[...]).astype(bf16) for r in halves]
    scores = [jnp.dot(h, wqk_ref[...], preferred_element_type=jnp.float32) for h in hs]
    for hh in range(X_HEADS):
        mc = slice(hh * m_len, (hh + 1) * m_len)
        for s, r in zip(scores, halves):
            p = jnp.exp(s[:, mc] - jnp.max(s[:, mc], axis=-1, keepdims=True))
            l = jnp.sum(p, axis=-1, keepdims=True)
            p_ref[r, mc] = (p * (1.0 / l)).astype(bf16)
    for r in halves:
        y = x_ref[r, :] + jnp.dot(p_ref[r, :], vxo_ref[...],
                                  preferred_element_type=jnp.float32)
        o_ref[r, :] = _rms(y, gf_ref[...])


def _stage(cols):
    rows = 1 << int(math.log2(STAGE_BYTES // (4 * cols)))
    return pltpu.VMEM((N_STAGE, rows, cols), jnp.float32)


def _resident(shape):
    return pl.BlockSpec((None,) + tuple(shape[1:]), lambda *_: (0,) * len(shape),
                        pipeline_mode=pl.Buffered(1))


def kernel(x, mem, norm_mix_g, w_in, conv_w, gm_ln_g, gm_ln_b, gm_ws, gm_bs, w_out,
           norm_x_g, norm_mem_g, w_q, w_kv, w_xo, norm_final_g):
    b, s, d = x.shape
    m_len = mem.shape[1]
    assert w_in.shape[0] == 1, "the final norm is fused into the (single) layer's attention call"
    assert s % TM_MIX == 0 and TM_MIX % MIX_SUB == 0 and MIX_SUB % CHUNK == 0 and d % CW == 0
    assert s % TM_ATTN == 0 and TM_ATTN % ATTN_HALF == 0
    bf16 = jnp.bfloat16
    f32 = jnp.float32
    params = pltpu.CompilerParams(
        dimension_semantics=("arbitrary", "arbitrary"), vmem_limit_bytes=VMEM_LIMIT)
    gain = lambda g: g.reshape(1, 1, d)
    hbm = pl.BlockSpec(memory_space=pl.ANY)
    tile = lambda tm: pl.BlockSpec((None, tm, d), lambda i, t: (i, t, 0))
    sems = pltpu.SemaphoreType.DMA((N_STAGE,))
    n_sub = TM_MIX // MIX_SUB

    x = pl.pallas_call(
        _mixer_kernel,
        grid=(b, s // TM_MIX),
        in_specs=[tile(TM_MIX),
                  _resident((1, 1, d)),
                  hbm,
                  _resident(conv_w.shape),
                  _resident((1, 1, d)), _resident((1, 1, d)),
                  _resident(gm_ws.shape),
                  _resident(gm_bs.shape),
                  hbm],
        out_specs=tile(TM_MIX),
        out_shape=jax.ShapeDtypeStruct((b, s, d), f32),
        scratch_shapes=[pltpu.VMEM((d, N_SECT * d), bf16),
                        pltpu.VMEM((2 * d, d), bf16),
                        _stage(N_SECT * d), _stage(d), sems,
                        pltpu.VMEM((n_sub, MIX_SUB, d), bf16),
                        pltpu.VMEM((2, MIX_SUB, N_SECT * CW), f32),
                        pltpu.VMEM((SUBLANES, d), f32),
                        pltpu.VMEM((n_sub, MIX_SUB, 2 * d), bf16)],
        compiler_params=params,
        name="mixer",
    )(x, gain(norm_mix_g), w_in[0], conv_w, gain(gm_ln_g), gain(gm_ln_b), gm_ws, gm_bs,
      w_out[0])

    return pl.pallas_call(
        _attn_kernel,
        grid=(b, s // TM_ATTN),
        in_specs=[tile(TM_ATTN),
                  pl.BlockSpec((None, m_len, d), lambda i, t: (i, 0, 0)),
                  _resident((1, 1, d)), _resident((1, 1, d)),
                  hbm, hbm, hbm,
                  _resident((1, 1, d))],
        out_specs=tile(TM_ATTN),
        out_shape=jax.ShapeDtypeStruct((b, s, d), f32),
        scratch_shapes=[pltpu.VMEM((d, d), bf16),
                        pltpu.VMEM((d, 2 * d), bf16),
                        pltpu.VMEM((d, d), bf16),
                        _stage(d), _stage(2 * d), sems,
                        pltpu.VMEM((d, X_HEADS * m_len), bf16),
                        pltpu.VMEM((X_HEADS * m_len, d), bf16),
                        pltpu.VMEM((TM_ATTN, X_HEADS * m_len), bf16)],
        compiler_params=params,
        name="xattn",
    )(x, mem, gain(norm_x_g), gain(norm_mem_g), w_q[0], w_kv[0], w_xo[0],
      gain(norm_final_g))
```

```python
import math

import jax
import jax.numpy as jnp
from jax import lax
from jax.experimental import pallas as pl
from jax.experimental.pallas import tpu as pltpu

EPS = 1e-6
CONV_K = 3
CHUNK = 128
HEAD_DIM = 128
N_SECT = 7
X_HEADS = 4
SUBLANES = 8

TM = 512
TM_ATTN = 1024
ATTN_HALF = 512
CW = 256
N_STAGE = 3
STAGE_BYTES = 1 << 21
VMEM_LIMIT = 56 * 1024 * 1024


def _rms(x, g):
    ms = jnp.mean(x * x, axis=-1, keepdims=True)
    return x * lax.rsqrt(ms + EPS) * g


def _sigmoid(z):
    return 0.5 * (1.0 + jnp.tanh(0.5 * z))


def _gelu_tanh(x):
    c = math.sqrt(2.0 / math.pi)
    return 0.5 * x * (1.0 + jnp.tanh(c * (x + 0.044715 * (x * x * x))))


def _load_weights_bf16(jobs, sem_ref):
    blocks = [(src, dst, stage, r)
              for src, dst, stage in jobs
              for r in range(0, src.shape[0], stage.shape[1])]

    def copy(k):
        src, _, stage, r = blocks[k]
        return pltpu.make_async_copy(src.at[pl.ds(r, stage.shape[1])],
                                     stage.at[k % N_STAGE], sem_ref.at[k % N_STAGE])

    for k in range(min(N_STAGE - 1, len(blocks))):
        copy(k).start()
    for k, (_, dst, stage, r) in enumerate(blocks):
        if k + N_STAGE - 1 < len(blocks):
            copy(k + N_STAGE - 1).start()
        copy(k).wait()
        dst[r:r + stage.shape[1], :] = stage[k % N_STAGE].astype(jnp.bfloat16)


def _first_grid_step():
    return (pl.program_id(0) == 0) & (pl.program_id(1) == 0)


def _mixer_kernel(x_ref, g_ref, w_in_hbm, conv_w_ref, ln_g_ref, ln_b_ref,
                  ws_ref, bst_ref, w_out_hbm, o_ref,
                  w_in_ref, w_out_ref, stage_in_ref, stage_out_ref, sem_ref,
                  h_ref, proj_ref, carry_ref, mix_ref):
    tm, d = x_ref.shape
    n_col_chunks = d // CW
    n_row_blocks = tm // CHUNK
    heads_per_chunk = CW // HEAD_DIM

    @pl.when(_first_grid_step())
    def _():
        _load_weights_bf16([(w_in_hbm, w_in_ref, stage_in_ref),
                            (w_out_hbm, w_out_ref, stage_out_ref)], sem_ref)

    @pl.when(pl.program_id(1) == 0)
    def _():
        carry_ref[...] = jnp.zeros_like(carry_ref)

    h_ref[...] = _rms(x_ref[...], g_ref[...]).astype(jnp.bfloat16)

    tri = (lax.broadcasted_iota(jnp.int32, (CHUNK, CHUNK), 0)
           >= lax.broadcasted_iota(jnp.int32, (CHUNK, CHUNK), 1))

    def in_proj(j):
        for s in range(N_SECT):
            proj_ref[j % 2, :, s * CW:(s + 1) * CW] = jnp.dot(
                h_ref[...], w_in_ref[:, s * d + j * CW:s * d + (j + 1) * CW],
                preferred_element_type=jnp.float32)

    def out_proj(c0, c1):
        part = [jnp.dot(mix_ref[:, o + c0:o + c1], w_out_ref[o + c0:o + c1, :],
                        preferred_element_type=jnp.float32) for o in (0, d)]
        return part[0] + part[1]

    in_proj(0)
    for j in range(n_col_chunks):
        cols = slice(j * CW, (j + 1) * CW)
        if j + 1 < n_col_chunks:
            in_proj(j + 1)
        else:
            o_ref[...] = x_ref[...] + out_proj(0, j * CW)

        def sect(s, rows, j=j):
            return proj_ref[j % 2, rows, s * CW:(s + 1) * CW]

        w0 = conv_w_ref[0:1, cols]
        w1 = conv_w_ref[1:2, cols]
        w2 = conv_w_ref[2:3, cols]
        wc = [jnp.where(tri, ws_ref[j * heads_per_chunk + hh], 0.0).astype(jnp.bfloat16)
              for hh in range(heads_per_chunk)]

        for r in range(n_row_blocks):
            rows = slice(r * CHUNK, (r + 1) * CHUNK)
            g = sect(1, rows) * sect(2, rows)
            if r == 0:
                prev = carry_ref[:, cols]
            else:
                prows = slice(r * CHUNK - SUBLANES, r * CHUNK)
                prev = sect(1, prows) * sect(2, prows)
            if r == n_row_blocks - 1:
                carry_ref[:, cols] = g[CHUNK - SUBLANES:, :]
            ext = jnp.concatenate([prev, g], axis=0)
            g1 = ext[SUBLANES - 1:SUBLANES - 1 + CHUNK, :]
            g2 = ext[SUBLANES - 2:SUBLANES - 2 + CHUNK, :]
            conv = w0 * g2 + w1 * g1 + w2 * g
            za = sect(3, rows)
            a = sect(0, rows) * conv * (za * _sigmoid(za))
            mix_ref[rows, cols] = a.astype(jnp.bfloat16)

            u = _gelu_tanh(sect(4, rows))
            v = _gelu_tanh(sect(5, rows))
            zb = sect(6, rows)
            sps = []
            for hh in range(heads_per_chunk):
                hc = slice(hh * HEAD_DIM, (hh + 1) * HEAD_DIM)
                gcols = slice(j * CW + hh * HEAD_DIM, j * CW + (hh + 1) * HEAD_DIM)
                vh = v[:, hc]
                mu = jnp.mean(vh, axis=-1, keepdims=True)
                vc = vh - mu
                var = jnp.mean(vc * vc, axis=-1, keepdims=True)
                vn = vc * lax.rsqrt(var + EPS) * ln_g_ref[:, gcols] + ln_b_ref[:, gcols]
                sp = jnp.dot(wc[hh], vn.astype(jnp.bfloat16),
                             preferred_element_type=jnp.float32)
                head = j * heads_per_chunk + hh
                sps.append(sp + bst_ref[:, head:head + 1])
            sp = jnp.concatenate(sps, axis=-1)
            bo = u * sp * (zb * _sigmoid(zb))
            mix_ref[rows, d + j * CW:d + (j + 1) * CW] = bo.astype(jnp.bfloat16)

    o_ref[...] += out_proj((n_col_chunks - 1) * CW, d)


def _attn_kernel(x_ref, mem_ref, gx_ref, gm_ref, w_q_hbm, w_kv_hbm, w_xo_hbm, gf_ref,
                 o_ref,
                 w_q_ref, w_kv_ref, w_xo_ref, stage_d_ref, stage_kv_ref, sem_ref,
                 wqk_ref, vxo_ref, p_ref):
    tm, d = x_ref.shape
    m_len = mem_ref.shape[0]
    hd = d // X_HEADS
    bf16 = jnp.bfloat16

    @pl.when(_first_grid_step())
    def _():
        _load_weights_bf16([(w_q_hbm, w_q_ref, stage_d_ref),
                            (w_kv_hbm, w_kv_ref, stage_kv_ref),
                            (w_xo_hbm, w_xo_ref, stage_d_ref)], sem_ref)

    @pl.when(pl.program_id(1) == 0)
    def _():
        m = _rms(mem_ref[...], gm_ref[...]).astype(bf16)
        kv = jnp.dot(m, w_kv_ref[...], preferred_element_type=jnp.float32)
        for hh in range(X_HEADS):
            hc = slice(hh * hd, (hh + 1) * hd)
            mc = slice(hh * m_len, (hh + 1) * m_len)
            k_h = kv[:, hc].astype(bf16)
            v_h = kv[:, d + hh * hd:d + (hh + 1) * hd].astype(bf16)
            wqk = lax.dot_general(w_q_ref[:, hc], k_h, (((1,), (1,)), ((), ())),
                                  preferred_element_type=jnp.float32)
            wqk_ref[:, mc] = (wqk * (1.0 / math.sqrt(hd))).astype(bf16)
            vxo_ref[mc, :] = jnp.dot(v_h, w_xo_ref[hc, :],
                                     preferred_element_type=jnp.float32).astype(bf16)

    halves = [slice(i * ATTN_HALF, (i + 1) * ATTN_HALF) for i in range(tm // ATTN_HALF)]
    hs = [_rms(x_ref[r, :], gx_ref[...]).astype(bf16) for r in halves]
    scores = [jnp.dot(h, wqk_ref[...], preferred_element_type=jnp.float32) for h in hs]
    for hh in range(X_HEADS):
        mc = slice(hh * m_len, (hh + 1) * m_len)
        for s, r in zip(scores, halves):
            p = jnp.exp(s[:, mc] - jnp.max(s[:, mc], axis=-1, keepdims=True))
            l = jnp.sum(p, axis=-1, keepdims=True)
            p_ref[r, mc] = (p * (1.0 / l)).astype(bf16)
    for r in halves:
        y = x_ref[r, :] + jnp.dot(p_ref[r, :], vxo_ref[...],
                                  preferred_element_type=jnp.float32)
        o_ref[r, :] = _rms(y, gf_ref[...])


def _stage(cols):
    rows = 1 << int(math.log2(STAGE_BYTES // (4 * cols)))
    return pltpu.VMEM((N_STAGE, rows, cols), jnp.float32)


def _resident(shape):
    return pl.BlockSpec(shape, lambda *_: (0,) * len(shape),
                        pipeline_mode=pl.Buffered(1))


def kernel(x, mem, norm_mix_g, w_in, conv_w, gm_ln_g, gm_ln_b, gm_ws, gm_bs, w_out,
           norm_x_g, norm_mem_g, w_q, w_kv, w_xo, norm_final_g):
    b, s, d = x.shape
    m_len = mem.shape[1]
    assert w_in.shape[0] == 1, "the final norm is fused into the (single) layer's attention call"
    assert s % TM == 0 and TM % CHUNK == 0 and d % CW == 0
    assert s % TM_ATTN == 0 and TM_ATTN % ATTN_HALF == 0
    bf16 = jnp.bfloat16
    f32 = jnp.float32
    params = pltpu.CompilerParams(
        dimension_semantics=("arbitrary", "arbitrary"), vmem_limit_bytes=VMEM_LIMIT)
    row = lambda a: a.reshape(1, -1)
    hbm = pl.BlockSpec(memory_space=pl.ANY)
    tile = pl.BlockSpec((None, TM, d), lambda i, t: (i, t, 0))
    sems = pltpu.SemaphoreType.DMA((N_STAGE,))

    x = pl.pallas_call(
        _mixer_kernel,
        grid=(b, s // TM),
        in_specs=[tile,
                  _resident((1, d)),
                  hbm,
                  _resident((CONV_K, d)),
                  _resident((1, d)), _resident((1, d)),
                  _resident(gm_ws.shape[1:]),
                  _resident((CHUNK, gm_bs.shape[1])),
                  hbm],
        out_specs=tile,
        out_shape=jax.ShapeDtypeStruct((b, s, d), f32),
        scratch_shapes=[pltpu.VMEM((d, N_SECT * d), bf16),
                        pltpu.VMEM((2 * d, d), bf16),
                        _stage(N_SECT * d), _stage(d), sems,
                        pltpu.VMEM((TM, d), bf16),
                        pltpu.VMEM((2, TM, N_SECT * CW), f32),
                        pltpu.VMEM((SUBLANES, d), f32),
                        pltpu.VMEM((TM, 2 * d), bf16)],
        compiler_params=params,
        name="mixer",
    )(x, row(norm_mix_g[0]), w_in[0], conv_w[0], row(gm_ln_g[0]), row(gm_ln_b[0]),
      gm_ws[0], gm_bs[0].T, w_out[0])

    attn_tile = pl.BlockSpec((None, TM_ATTN, d), lambda i, t: (i, t, 0))
    return pl.pallas_call(
        _attn_kernel,
        grid=(b, s // TM_ATTN),
        in_specs=[attn_tile,
                  pl.BlockSpec((None, m_len, d), lambda i, t: (i, 0, 0)),
                  _resident((1, d)), _resident((1, d)),
                  hbm, hbm, hbm,
                  _resident((1, d))],
        out_specs=attn_tile,
        out_shape=jax.ShapeDtypeStruct((b, s, d), f32),
        scratch_shapes=[pltpu.VMEM((d, d), bf16),
                        pltpu.VMEM((d, 2 * d), bf16),
                        pltpu.VMEM((d, d), bf16),
                        _stage(d), _stage(2 * d), sems,
                        pltpu.VMEM((d, X_HEADS * m_len), bf16),
                        pltpu.VMEM((X_HEADS * m_len, d), bf16),
                        pltpu.VMEM((TM_ATTN, X_HEADS * m_len), bf16)],
        compiler_params=params,
        name="xattn",
    )(x, mem, row(norm_x_g[0]), row(norm_mem_g[0]), w_q[0], w_kv[0], w_xo[0],
      row(norm_final_g))
```

```python
import math

import jax
import jax.numpy as jnp
from jax import lax
from jax.experimental import pallas as pl
from jax.experimental.pallas import tpu as pltpu

EPS = 1e-6
CONV_K = 3
CHUNK = 128
HEAD_DIM = 128
N_SECT = 7
X_HEADS = 4
SUBLANES = 8

TM = 512
TM_ATTN = 1024
ATTN_HALF = 512
CW = 256
N_STAGE = 3
STAGE_BYTES = 1 << 21
VMEM_LIMIT = 56 * 1024 * 1024


def _rms(x, g):
    ms = jnp.mean(x * x, axis=-1, keepdims=True)
    return x * lax.rsqrt(ms + EPS) * g


def _silu(z):
    zh = 0.5 * z
    return zh + zh * jnp.tanh(zh)


def _gelu_tanh(x):
    c = math.sqrt(2.0 / math.pi)
    xh = 0.5 * x
    return xh + xh * jnp.tanh(x * (c + (c * 0.044715) * (x * x)))


def _load_weights_bf16(jobs, sem_ref):
    blocks = [(src, dst, stage, r)
              for src, dst, stage in jobs
              for r in range(0, src.shape[0], stage.shape[1])]

    def copy(k):
        src, _, stage, r = blocks[k]
        return pltpu.make_async_copy(src.at[pl.ds(r, stage.shape[1])],
                                     stage.at[k % N_STAGE], sem_ref.at[k % N_STAGE])

    for k in range(min(N_STAGE - 1, len(blocks))):
        copy(k).start()
    for k, (_, dst, stage, r) in enumerate(blocks):
        if k + N_STAGE - 1 < len(blocks):
            copy(k + N_STAGE - 1).start()
        copy(k).wait()
        dst[r:r + stage.shape[1], :] = stage[k % N_STAGE].astype(jnp.bfloat16)


def _first_grid_step():
    return (pl.program_id(0) == 0) & (pl.program_id(1) == 0)


def _mixer_kernel(x_ref, g_ref, w_in_hbm, conv_w_ref, ln_g_ref, ln_b_ref,
                  ws_ref, bst_ref, w_out_hbm, o_ref,
                  w_in_ref, w_out_ref, stage_in_ref, stage_out_ref, sem_ref,
                  h_ref, proj_ref, carry_ref, mix_ref):
    tm, d = x_ref.shape
    n_col_chunks = d // CW
    n_row_blocks = tm // CHUNK
    heads_per_chunk = CW // HEAD_DIM

    @pl.when(_first_grid_step())
    def _():
        _load_weights_bf16([(w_in_hbm, w_in_ref, stage_in_ref),
                            (w_out_hbm, w_out_ref, stage_out_ref)], sem_ref)

    @pl.when(pl.program_id(1) == 0)
    def _():
        carry_ref[...] = jnp.zeros_like(carry_ref)

    h_ref[...] = _rms(x_ref[...], g_ref[...]).astype(jnp.bfloat16)

    tri = (lax.broadcasted_iota(jnp.int32, (CHUNK, CHUNK), 0)
           >= lax.broadcasted_iota(jnp.int32, (CHUNK, CHUNK), 1))

    def in_proj(j):
        for s in range(N_SECT):
            proj_ref[j % 2, :, s * CW:(s + 1) * CW] = jnp.dot(
                h_ref[...], w_in_ref[:, s * d + j * CW:s * d + (j + 1) * CW],
                preferred_element_type=jnp.float32)

    def out_proj(c0, c1):
        part = [jnp.dot(mix_ref[:, o + c0:o + c1], w_out_ref[o + c0:o + c1, :],
                        preferred_element_type=jnp.float32) for o in (0, d)]
        return part[0] + part[1]

    in_proj(0)
    for j in range(n_col_chunks):
        if j + 1 < n_col_chunks:
            in_proj(j + 1)
        else:
            o_ref[...] = x_ref[...] + out_proj(0, j * CW)

        wc = [jnp.where(tri, ws_ref[j * heads_per_chunk + hh], 0.0).astype(jnp.bfloat16)
              for hh in range(heads_per_chunk)]

        for r in range(n_row_blocks):
            rows = slice(r * CHUNK, (r + 1) * CHUNK)
            prows = slice(r * CHUNK - SUBLANES, r * CHUNK)
            for hh in range(heads_per_chunk):
                hc = slice(hh * HEAD_DIM, (hh + 1) * HEAD_DIM)
                gc = slice(j * CW + hh * HEAD_DIM, j * CW + (hh + 1) * HEAD_DIM)

                def sect(s, rws, hc=hc, j=j):
                    return proj_ref[j % 2, rws, s * CW + hc.start:s * CW + hc.stop]

                vh = _gelu_tanh(sect(5, rows))
                mu = jnp.mean(vh, axis=-1, keepdims=True)
                vc = vh - mu
                var = jnp.mean(vc * vc, axis=-1, keepdims=True)
                vn = vc * lax.rsqrt(var + EPS) * ln_g_ref[:, gc] + ln_b_ref[:, gc]
                sp = jnp.dot(wc[hh], vn.astype(jnp.bfloat16),
                             preferred_element_type=jnp.float32)
                head = j * heads_per_chunk + hh
                bo = (_gelu_tanh(sect(4, rows)) * (sp + bst_ref[:, head:head + 1])
                      * _silu(sect(6, rows)))
                mix_ref[rows, d + gc.start:d + gc.stop] = bo.astype(jnp.bfloat16)

                g = sect(1, rows) * sect(2, rows)
                if r == 0:
                    prev = carry_ref[:, gc]
                else:
                    prev = sect(1, prows) * sect(2, prows)
                if r == n_row_blocks - 1:
                    carry_ref[:, gc] = g[CHUNK - SUBLANES:, :]
                ext = jnp.concatenate([prev, g], axis=0)
                g1 = ext[SUBLANES - 1:SUBLANES - 1 + CHUNK, :]
                g2 = ext[SUBLANES - 2:SUBLANES - 2 + CHUNK, :]
                conv = (conv_w_ref[0:1, gc] * g2 + conv_w_ref[1:2, gc] * g1
                        + conv_w_ref[2:3, gc] * g)
                a = sect(0, rows) * conv * _silu(sect(3, rows))
                mix_ref[rows, gc] = a.astype(jnp.bfloat16)

    o_ref[...] += out_proj((n_col_chunks - 1) * CW, d)


def _attn_kernel(x_ref, mem_ref, gx_ref, gm_ref, w_q_hbm, w_kv_hbm, w_xo_hbm, gf_ref,
                 o_ref,
                 w_q_ref, w_kv_ref, w_xo_ref, stage_d_ref, stage_kv_ref, sem_ref,
                 wqk_ref, vxo_ref, p_ref):
    tm, d = x_ref.shape
    m_len = mem_ref.shape[0]
    hd = d // X_HEADS
    bf16 = jnp.bfloat16

    @pl.when(_first_grid_step())
    def _():
        _load_weights_bf16([(w_q_hbm, w_q_ref, stage_d_ref),
                            (w_kv_hbm, w_kv_ref, stage_kv_ref),
                            (w_xo_hbm, w_xo_ref, stage_d_ref)], sem_ref)

    @pl.when(pl.program_id(1) == 0)
    def _():
        m = _rms(mem_ref[...], gm_ref[...]).astype(bf16)
        kv = jnp.dot(m, w_kv_ref[...], preferred_element_type=jnp.float32)
        for hh in range(X_HEADS):
            hc = slice(hh * hd, (hh + 1) * hd)
            mc = slice(hh * m_len, (hh + 1) * m_len)
            k_h = kv[:, hc].astype(bf16)
            v_h = kv[:, d + hh * hd:d + (hh + 1) * hd].astype(bf16)
            wqk = lax.dot_general(w_q_ref[:, hc], k_h, (((1,), (1,)), ((), ())),
                                  preferred_element_type=jnp.float32)
            wqk_ref[:, mc] = (wqk * (1.0 / math.sqrt(hd))).astype(bf16)
            vxo_ref[mc, :] = jnp.dot(v_h, w_xo_ref[hc, :],
                                     preferred_element_type=jnp.float32).astype(bf16)

    halves = [slice(i * ATTN_HALF, (i + 1) * ATTN_HALF) for i in range(tm // ATTN_HALF)]
    hs = [_rms(x_ref[r, :], gx_ref[...]).astype(bf16) for r in halves]
    scores = [jnp.dot(h, wqk_ref[...], preferred_element_type=jnp.float32) for h in hs]
    for hh in range(X_HEADS):
        mc = slice(hh * m_len, (hh + 1) * m_len)
        for s, r in zip(scores, halves):
            p = jnp.exp(s[:, mc] - jnp.max(s[:, mc], axis=-1, keepdims=True))
            l = jnp.sum(p, axis=-1, keepdims=True)
            p_ref[r, mc] = (p * (1.0 / l)).astype(bf16)
    for r in halves:
        y = x_ref[r, :] + jnp.dot(p_ref[r, :], vxo_ref[...],
                                  preferred_element_type=jnp.float32)
        o_ref[r, :] = _rms(y, gf_ref[...])


def _stage(cols):
    rows = 1 << int(math.log2(STAGE_BYTES // (4 * cols)))
    return pltpu.VMEM((N_STAGE, rows, cols), jnp.float32)


def _resident(shape):
    return pl.BlockSpec(shape, lambda *_: (0,) * len(shape),
                        pipeline_mode=pl.Buffered(1))


def kernel(x, mem, norm_mix_g, w_in, conv_w, gm_ln_g, gm_ln_b, gm_ws, gm_bs, w_out,
           norm_x_g, norm_mem_g, w_q, w_kv, w_xo, norm_final_g):
    b, s, d = x.shape
    m_len = mem.shape[1]
    assert w_in.shape[0] == 1, "the final norm is fused into the (single) layer's attention call"
    assert s % TM == 0 and TM % CHUNK == 0 and d % CW == 0
    assert s % TM_ATTN == 0 and TM_ATTN % ATTN_HALF == 0
    bf16 = jnp.bfloat16
    f32 = jnp.float32
    params = pltpu.CompilerParams(
        dimension_semantics=("arbitrary", "arbitrary"), vmem_limit_bytes=VMEM_LIMIT)
    row = lambda a: a.reshape(1, -1)
    hbm = pl.BlockSpec(memory_space=pl.ANY)
    tile = pl.BlockSpec((None, TM, d), lambda i, t: (i, t, 0))
    sems = pltpu.SemaphoreType.DMA((N_STAGE,))

    x = pl.pallas_call(
        _mixer_kernel,
        grid=(b, s // TM),
        in_specs=[tile,
                  _resident((1, d)),
                  hbm,
                  _resident((CONV_K, d)),
                  _resident((1, d)), _resident((1, d)),
                  _resident(gm_ws.shape[1:]),
                  _resident((CHUNK, gm_bs.shape[1])),
                  hbm],
        out_specs=tile,
        out_shape=jax.ShapeDtypeStruct((b, s, d), f32),
        scratch_shapes=[pltpu.VMEM((d, N_SECT * d), bf16),
                        pltpu.VMEM((2 * d, d), bf16),
                        _stage(N_SECT * d), _stage(d), sems,
                        pltpu.VMEM((TM, d), bf16),
                        pltpu.VMEM((2, TM, N_SECT * CW), f32),
                        pltpu.VMEM((SUBLANES, d), f32),
                        pltpu.VMEM((TM, 2 * d), bf16)],
        compiler_params=params,
        name="mixer",
    )(x, row(norm_mix_g[0]), w_in[0], conv_w[0], row(gm_ln_g[0]), row(gm_ln_b[0]),
      gm_ws[0], gm_bs[0].T, w_out[0])

    attn_tile = pl.BlockSpec((None, TM_ATTN, d), lambda i, t: (i, t, 0))
    return pl.pallas_call(
        _attn_kernel,
        grid=(b, s // TM_ATTN),
        in_specs=[attn_tile,
                  pl.BlockSpec((None, m_len, d), lambda i, t: (i, 0, 0)),
                  _resident((1, d)), _resident((1, d)),
                  hbm, hbm, hbm,
                  _resident((1, d))],
        out_specs=attn_tile,
        out_shape=jax.ShapeDtypeStruct((b, s, d), f32),
        scratch_shapes=[pltpu.VMEM((d, d), bf16),
                        pltpu.VMEM((d, 2 * d), bf16),
                        pltpu.VMEM((d, d), bf16),
                        _stage(d), _stage(2 * d), sems,
                        pltpu.VMEM((d, X_HEADS * m_len), bf16),
                        pltpu.VMEM((X_HEADS * m_len, d), bf16),
                        pltpu.VMEM((TM_ATTN, X_HEADS * m_len), bf16)],
        compiler_params=params,
        name="xattn",
    )(x, mem, row(norm_x_g[0]), row(norm_mem_g[0]), w_q[0], w_kv[0], w_xo[0],
      row(norm_final_g))
```

```python
import math

import jax
import jax.numpy as jnp
from jax import lax
from jax.experimental import pallas as pl
from jax.experimental.pallas import tpu as pltpu

EPS = 1e-6
CONV_K = 3
CHUNK = 128
HEAD_DIM = 128
N_SECT = 7
X_HEADS = 4
SUBLANES = 8

TM = 512
TM_ATTN = 1024
ATTN_HALF = 256
CW = 256
N_STAGE = 3
STAGE_BYTES = 1 << 21
VMEM_LIMIT = 56 * 1024 * 1024


def _rms(x, g):
    ms = jnp.mean(x * x, axis=-1, keepdims=True)
    return x * lax.rsqrt(ms + EPS) * g


def _sigmoid(z):
    return 0.5 * (1.0 + jnp.tanh(0.5 * z))


def _gelu_tanh(x):
    c = math.sqrt(2.0 / math.pi)
    return 0.5 * x * (1.0 + jnp.tanh(c * (x + 0.044715 * (x * x * x))))


def _load_weights_bf16(jobs, sem_ref):
    blocks = [(src, dst, stage, r)
              for src, dst, stage in jobs
              for r in range(0, src.shape[0], stage.shape[1])]

    def copy(k):
        src, _, stage, r = blocks[k]
        return pltpu.make_async_copy(src.at[pl.ds(r, stage.shape[1])],
                                     stage.at[k % N_STAGE], sem_ref.at[k % N_STAGE])

    for k in range(min(N_STAGE - 1, len(blocks))):
        copy(k).start()
    for k, (_, dst, stage, r) in enumerate(blocks):
        if k + N_STAGE - 1 < len(blocks):
            copy(k + N_STAGE - 1).start()
        copy(k).wait()
        dst[r:r + stage.shape[1], :] = stage[k % N_STAGE].astype(jnp.bfloat16)


def _first_grid_step():
    return (pl.program_id(0) == 0) & (pl.program_id(1) == 0)


def _mixer_kernel(x_ref, g_ref, w_in_hbm, conv_w_ref, ln_g_ref, ln_b_ref,
                  ws_ref, bs_ref, w_out_hbm, o_ref,
                  w_in_ref, w_out_ref, stage_in_ref, stage_out_ref, sem_ref,
                  h_ref, proj_ref, carry_ref, mix_ref, bst_ref):
    tm, d = x_ref.shape
    n_col_chunks = d // CW
    n_row_blocks = tm // CHUNK
    heads_per_chunk = CW // HEAD_DIM

    @pl.when(_first_grid_step())
    def _():
        _load_weights_bf16([(w_in_hbm, w_in_ref, stage_in_ref),
                            (w_out_hbm, w_out_ref, stage_out_ref)], sem_ref)

    @pl.when(pl.program_id(1) == 0)
    def _():
        carry_ref[...] = jnp.zeros_like(carry_ref)

    h_ref[...] = _rms(x_ref[...], g_ref[...]).astype(jnp.bfloat16)
    bst_ref[...] = bs_ref[...].T

    tri = (lax.broadcasted_iota(jnp.int32, (CHUNK, CHUNK), 0)
           >= lax.broadcasted_iota(jnp.int32, (CHUNK, CHUNK), 1))

    def in_proj(j):
        for s in range(N_SECT):
            proj_ref[j % 2, :, s * CW:(s + 1) * CW] = jnp.dot(
                h_ref[...], w_in_ref[:, s * d + j * CW:s * d + (j + 1) * CW],
                preferred_element_type=jnp.float32)

    def out_proj(c0, c1):
        part = [jnp.dot(mix_ref[:, o + c0:o + c1], w_out_ref[o + c0:o + c1, :],
                        preferred_element_type=jnp.float32) for o in (0, d)]
        return part[0] + part[1]

    in_proj(0)
    for j in range(n_col_chunks):
        cols = slice(j * CW, (j + 1) * CW)
        if j + 1 < n_col_chunks:
            in_proj(j + 1)
        else:
            o_ref[...] = x_ref[...] + out_proj(0, j * CW)

        def sect(s, rows, j=j):
            return proj_ref[j % 2, rows, s * CW:(s + 1) * CW]

        w0 = conv_w_ref[0:1, cols]
        w1 = conv_w_ref[1:2, cols]
        w2 = conv_w_ref[2:3, cols]
        wc = [jnp.where(tri, ws_ref[j * heads_per_chunk + hh], 0.0).astype(jnp.bfloat16)
              for hh in range(heads_per_chunk)]

        for r in range(n_row_blocks):
            rows = slice(r * CHUNK, (r + 1) * CHUNK)
            g = sect(1, rows) * sect(2, rows)
            if r == 0:
                prev = carry_ref[:, cols]
            else:
                prows = slice(r * CHUNK - SUBLANES, r * CHUNK)
                prev = sect(1, prows) * sect(2, prows)
            if r == n_row_blocks - 1:
                carry_ref[:, cols] = g[CHUNK - SUBLANES:, :]
            ext = jnp.concatenate([prev, g], axis=0)
            g1 = ext[SUBLANES - 1:SUBLANES - 1 + CHUNK, :]
            g2 = ext[SUBLANES - 2:SUBLANES - 2 + CHUNK, :]
            conv = w0 * g2 + w1 * g1 + w2 * g
            za = sect(3, rows)
            a = sect(0, rows) * conv * (za * _sigmoid(za))
            mix_ref[rows, cols] = a.astype(jnp.bfloat16)

            u = _gelu_tanh(sect(4, rows))
            v = _gelu_tanh(sect(5, rows))
            zb = sect(6, rows)
            sps = []
            for hh in range(heads_per_chunk):
                hc = slice(hh * HEAD_DIM, (hh + 1) * HEAD_DIM)
                gcols = slice(j * CW + hh * HEAD_DIM, j * CW + (hh + 1) * HEAD_DIM)
                vh = v[:, hc]
                mu = jnp.mean(vh, axis=-1, keepdims=True)
                vc = vh - mu
                var = jnp.mean(vc * vc, axis=-1, keepdims=True)
                vn = vc * lax.rsqrt(var + EPS) * ln_g_ref[:, gcols] + ln_b_ref[:, gcols]
                sp = jnp.dot(wc[hh], vn.astype(jnp.bfloat16),
                             preferred_element_type=jnp.float32)
                head = j * heads_per_chunk + hh
                sps.append(sp + bst_ref[:, head:head + 1])
            sp = jnp.concatenate(sps, axis=-1)
            bo = u * sp * (zb * _sigmoid(zb))
            mix_ref[rows, d + j * CW:d + (j + 1) * CW] = bo.astype(jnp.bfloat16)

    o_ref[...] += out_proj((n_col_chunks - 1) * CW, d)


def _attn_kernel(x_ref, mem_ref, gx_ref, gm_ref, w_q_hbm, w_kv_hbm, w_xo_hbm, gf_ref,
                 o_ref,
                 w_q_ref, w_kv_ref, w_xo_ref, stage_d_ref, stage_kv_ref, sem_ref,
                 wqk_ref, vxo_ref, p_ref):
    tm, d = x_ref.shape
    m_len = mem_ref.shape[0]
    hd = d // X_HEADS
    bf16 = jnp.bfloat16

    @pl.when(_first_grid_step())
    def _():
        _load_weights_bf16([(w_q_hbm, w_q_ref, stage_d_ref),
                            (w_kv_hbm, w_kv_ref, stage_kv_ref),
                            (w_xo_hbm, w_xo_ref, stage_d_ref)], sem_ref)

    @pl.when(pl.program_id(1) == 0)
    def _():
        m = _rms(mem_ref[...], gm_ref[...]).astype(bf16)
        kv = jnp.dot(m, w_kv_ref[...], preferred_element_type=jnp.float32)
        for hh in range(X_HEADS):
            hc = slice(hh * hd, (hh + 1) * hd)
            mc = slice(hh * m_len, (hh + 1) * m_len)
            k_h = kv[:, hc].astype(bf16)
            v_h = kv[:, d + hh * hd:d + (hh + 1) * hd].astype(bf16)
            wqk = lax.dot_general(w_q_ref[:, hc], k_h, (((1,), (1,)), ((), ())),
                                  preferred_element_type=jnp.float32)
            wqk_ref[:, mc] = (wqk * (1.0 / math.sqrt(hd))).astype(bf16)
            vxo_ref[mc, :] = jnp.dot(v_h, w_xo_ref[hc, :],
                                     preferred_element_type=jnp.float32).astype(bf16)

    halves = [slice(i * ATTN_HALF, (i + 1) * ATTN_HALF) for i in range(tm // ATTN_HALF)]
    hs = [_rms(x_ref[r, :], gx_ref[...]).astype(bf16) for r in halves]
    scores = [jnp.dot(h, wqk_ref[...], preferred_element_type=jnp.float32) for h in hs]
    for hh in range(X_HEADS):
        mc = slice(hh * m_len, (hh + 1) * m_len)
        for s, r in zip(scores, halves):
            p = jnp.exp(s[:, mc] - jnp.max(s[:, mc], axis=-1, keepdims=True))
            l = jnp.sum(p, axis=-1, keepdims=True)
            p_ref[r, mc] = (p * (1.0 / l)).astype(bf16)
    for r in halves:
        y = x_ref[r, :] + jnp.dot(p_ref[r, :], vxo_ref[...],
                                  preferred_element_type=jnp.float32)
        o_ref[r, :] = _rms(y, gf_ref[...])


def _stage(cols):
    rows = 1 << int(math.log2(STAGE_BYTES // (4 * cols)))
    return pltpu.VMEM((N_STAGE, rows, cols), jnp.float32)


def _resident(shape):
    return pl.BlockSpec(shape, lambda *_: (0,) * len(shape),
                        pipeline_mode=pl.Buffered(1))


def _resident_layer(shape):
    return pl.BlockSpec((None,) + tuple(shape[1:]), lambda *_: (0,) * len(shape),
                        pipeline_mode=pl.Buffered(1))


def kernel(x, mem, norm_mix_g, w_in, conv_w, gm_ln_g, gm_ln_b, gm_ws, gm_bs, w_out,
           norm_x_g, norm_mem_g, w_q, w_kv, w_xo, norm_final_g):
    b, s, d = x.shape
    m_len = mem.shape[1]
    assert w_in.shape[0] == 1, "the final norm is fused into the (single) layer's attention call"
    assert s % TM == 0 and TM % CHUNK == 0 and d % CW == 0
    assert s % TM_ATTN == 0 and TM_ATTN % ATTN_HALF == 0
    bf16 = jnp.bfloat16
    f32 = jnp.float32
    params = pltpu.CompilerParams(
        dimension_semantics=("arbitrary", "arbitrary"), vmem_limit_bytes=VMEM_LIMIT)
    row = lambda a: a.reshape(1, -1)
    hbm = pl.BlockSpec(memory_space=pl.ANY)
    tile = pl.BlockSpec((None, TM, d), lambda i, t: (i, t, 0))
    sems = pltpu.SemaphoreType.DMA((N_STAGE,))

    x = pl.pallas_call(
        _mixer_kernel,
        grid=(b, s // TM),
        in_specs=[tile,
                  _resident((1, d)),
                  hbm,
                  _resident_layer(conv_w.shape),
                  _resident((1, d)), _resident((1, d)),
                  _resident(gm_ws.shape[1:]),
                  _resident_layer(gm_bs.shape),
                  hbm],
        out_specs=tile,
        out_shape=jax.ShapeDtypeStruct((b, s, d), f32),
        scratch_shapes=[pltpu.VMEM((d, N_SECT * d), bf16),
                        pltpu.VMEM((2 * d, d), bf16),
                        _stage(N_SECT * d), _stage(d), sems,
                        pltpu.VMEM((TM, d), bf16),
                        pltpu.VMEM((2, TM, N_SECT * CW), f32),
                        pltpu.VMEM((SUBLANES, d), f32),
                        pltpu.VMEM((TM, 2 * d), bf16),
                        pltpu.VMEM((CHUNK, gm_bs.shape[1]), f32)],
        compiler_params=params,
        name="mixer",
    )(x, row(norm_mix_g[0]), w_in[0], conv_w, row(gm_ln_g[0]), row(gm_ln_b[0]),
      gm_ws[0], gm_bs, w_out[0])

    attn_tile = pl.BlockSpec((None, TM_ATTN, d), lambda i, t: (i, t, 0))
    return pl.pallas_call(
        _attn_kernel,
        grid=(b, s // TM_ATTN),
        in_specs=[attn_tile,
                  pl.BlockSpec((None, m_len, d), lambda i, t: (i, 0, 0)),
                  _resident((1, d)), _resident((1, d)),
                  hbm, hbm, hbm,
                  _resident((1, d))],
        out_specs=attn_tile,
        out_shape=jax.ShapeDtypeStruct((b, s, d), f32),
        scratch_shapes=[pltpu.VMEM((d, d), bf16),
                        pltpu.VMEM((d, 2 * d), bf16),
                        pltpu.VMEM((d, d), bf16),
                        _stage(d), _stage(2 * d), sems,
                        pltpu.VMEM((d, X_HEADS * m_len), bf16),
                        pltpu.VMEM((X_HEADS * m_len, d), bf16),
                        pltpu.VMEM((TM_ATTN, X_HEADS * m_len), bf16)],
        compiler_params=params,
        name="xattn",
    )(x, mem, row(norm_x_g[0]), row(norm_mem_g[0]), w_q[0], w_kv[0], w_xo[0],
      row(norm_final_g))
```

```python
import math

import jax
import jax.numpy as jnp
from jax import lax
from jax.experimental import pallas as pl
from jax.experimental.pallas import tpu as pltpu

EPS = 1e-6
CONV_K = 3
CHUNK = 128
HEAD_DIM = 128
N_SECT = 7
X_HEADS = 4
SUBLANES = 8

TM = 512
TM_ATTN = 1024
ATTN_HALF = 512
CW = 256
N_STAGE = 3
STAGE_BYTES = 1 << 21
VMEM_LIMIT = 56 * 1024 * 1024


def _rms(x, g):
    ms = jnp.mean(x * x, axis=-1, keepdims=True)
    return x * lax.rsqrt(ms + EPS) * g


def _sigmoid(z):
    return 0.5 * (1.0 + jnp.tanh(0.5 * z))


def _gelu_tanh(x):
    c = math.sqrt(2.0 / math.pi)
    return 0.5 * x * (1.0 + jnp.tanh(c * (x + 0.044715 * (x * x * x))))


def _load_weights_bf16(jobs, sem_ref):
    blocks = [(src, dst, stage, r)
              for src, dst, stage in jobs
              for r in range(0, src.shape[0], stage.shape[1])]

    def copy(k):
        src, _, stage, r = blocks[k]
        return pltpu.make_async_copy(src.at[pl.ds(r, stage.shape[1])],
                                     stage.at[k % N_STAGE], sem_ref.at[k % N_STAGE])

    for k in range(min(N_STAGE - 1, len(blocks))):
        copy(k).start()
    for k, (_, dst, stage, r) in enumerate(blocks):
        if k + N_STAGE - 1 < len(blocks):
            copy(k + N_STAGE - 1).start()
        copy(k).wait()
        dst[r:r + stage.shape[1], :] = stage[k % N_STAGE].astype(jnp.bfloat16)


def _first_grid_step():
    return (pl.program_id(0) == 0) & (pl.program_id(1) == 0)


def _mixer_kernel(x_ref, g_ref, w_in_hbm, conv_w_hbm, ln_g_ref, ln_b_ref,
                  ws_ref, bs_ref, w_out_hbm, o_ref,
                  w_in_ref, w_out_ref, stage_in_ref, stage_out_ref, sem_ref,
                  h_ref, proj_ref, carry_ref, mix_ref, bst_ref, conv_w_ref, conv_sem):
    tm, d = x_ref.shape
    n_col_chunks = d // CW
    n_row_blocks = tm // CHUNK
    heads_per_chunk = CW // HEAD_DIM

    @pl.when(_first_grid_step())
    def _():
        conv_copy = pltpu.make_async_copy(conv_w_hbm.at[0], conv_w_ref, conv_sem)
        conv_copy.start()
        _load_weights_bf16([(w_in_hbm, w_in_ref, stage_in_ref),
                            (w_out_hbm, w_out_ref, stage_out_ref)], sem_ref)
        conv_copy.wait()

    @pl.when(pl.program_id(1) == 0)
    def _():
        carry_ref[...] = jnp.zeros_like(carry_ref)

    h_ref[...] = _rms(x_ref[...], g_ref[...]).astype(jnp.bfloat16)
    bst_ref[...] = bs_ref[...].T

    tri = (lax.broadcasted_iota(jnp.int32, (CHUNK, CHUNK), 0)
           >= lax.broadcasted_iota(jnp.int32, (CHUNK, CHUNK), 1))

    def in_proj(j):
        for s in range(N_SECT):
            proj_ref[j % 2, :, s * CW:(s + 1) * CW] = jnp.dot(
                h_ref[...], w_in_ref[:, s * d + j * CW:s * d + (j + 1) * CW],
                preferred_element_type=jnp.float32)

    def out_proj(c0, c1):
        part = [jnp.dot(mix_ref[:, o + c0:o + c1], w_out_ref[o + c0:o + c1, :],
                        preferred_element_type=jnp.float32) for o in (0, d)]
        return part[0] + part[1]

    in_proj(0)
    for j in range(n_col_chunks):
        cols = slice(j * CW, (j + 1) * CW)
        if j + 1 < n_col_chunks:
            in_proj(j + 1)
        else:
            o_ref[...] = x_ref[...] + out_proj(0, j * CW)

        def sect(s, rows, j=j):
            return proj_ref[j % 2, rows, s * CW:(s + 1) * CW]

        w0 = conv_w_ref[0:1, cols]
        w1 = conv_w_ref[1:2, cols]
        w2 = conv_w_ref[2:3, cols]
        wc = [jnp.where(tri, ws_ref[j * heads_per_chunk + hh], 0.0).astype(jnp.bfloat16)
              for hh in range(heads_per_chunk)]

        for r in range(n_row_blocks):
            rows = slice(r * CHUNK, (r + 1) * CHUNK)
            g = sect(1, rows) * sect(2, rows)
            if r == 0:
                prev = carry_ref[:, cols]
            else:
                prows = slice(r * CHUNK - SUBLANES, r * CHUNK)
                prev = sect(1, prows) * sect(2, prows)
            if r == n_row_blocks - 1:
                carry_ref[:, cols] = g[CHUNK - SUBLANES:, :]
            ext = jnp.concatenate([prev, g], axis=0)
            g1 = ext[SUBLANES - 1:SUBLANES - 1 + CHUNK, :]
            g2 = ext[SUBLANES - 2:SUBLANES - 2 + CHUNK, :]
            conv = w0 * g2 + w1 * g1 + w2 * g
            za = sect(3, rows)
            a = sect(0, rows) * conv * (za * _sigmoid(za))
            mix_ref[rows, cols] = a.astype(jnp.bfloat16)

            u = _gelu_tanh(sect(4, rows))
            v = _gelu_tanh(sect(5, rows))
            zb = sect(6, rows)
            sps = []
            for hh in range(heads_per_chunk):
                hc = slice(hh * HEAD_DIM, (hh + 1) * HEAD_DIM)
                gcols = slice(j * CW + hh * HEAD_DIM, j * CW + (hh + 1) * HEAD_DIM)
                vh = v[:, hc]
                mu = jnp.mean(vh, axis=-1, keepdims=True)
                vc = vh - mu
                var = jnp.mean(vc * vc, axis=-1, keepdims=True)
                vn = vc * lax.rsqrt(var + EPS) * ln_g_ref[:, gcols] + ln_b_ref[:, gcols]
                sp = jnp.dot(wc[hh], vn.astype(jnp.bfloat16),
                             preferred_element_type=jnp.float32)
                head = j * heads_per_chunk + hh
                sps.append(sp + bst_ref[:, head:head + 1])
            sp = jnp.concatenate(sps, axis=-1)
            bo = u * sp * (zb * _sigmoid(zb))
            mix_ref[rows, d + j * CW:d + (j + 1) * CW] = bo.astype(jnp.bfloat16)

    o_ref[...] += out_proj((n_col_chunks - 1) * CW, d)


def _attn_kernel(x_ref, mem_ref, gx_ref, gm_ref, w_q_hbm, w_kv_hbm, w_xo_hbm, gf_ref,
                 o_ref,
                 w_q_ref, w_kv_ref, w_xo_ref, stage_d_ref, stage_kv_ref, sem_ref,
                 wqk_ref, vxo_ref, p_ref):
    tm, d = x_ref.shape
    m_len = mem_ref.shape[0]
    hd = d // X_HEADS
    bf16 = jnp.bfloat16

    @pl.when(_first_grid_step())
    def _():
        _load_weights_bf16([(w_q_hbm, w_q_ref, stage_d_ref),
                            (w_kv_hbm, w_kv_ref, stage_kv_ref),
                            (w_xo_hbm, w_xo_ref, stage_d_ref)], sem_ref)

    @pl.when(pl.program_id(1) == 0)
    def _():
        m = _rms(mem_ref[...], gm_ref[...]).astype(bf16)
        kv = jnp.dot(m, w_kv_ref[...], preferred_element_type=jnp.float32)
        for hh in range(X_HEADS):
            hc = slice(hh * hd, (hh + 1) * hd)
            mc = slice(hh * m_len, (hh + 1) * m_len)
            k_h = kv[:, hc].astype(bf16)
            v_h = kv[:, d + hh * hd:d + (hh + 1) * hd].astype(bf16)
            wqk = lax.dot_general(w_q_ref[:, hc], k_h, (((1,), (1,)), ((), ())),
                                  preferred_element_type=jnp.float32)
            wqk_ref[:, mc] = (wqk * (1.0 / math.sqrt(hd))).astype(bf16)
            vxo_ref[mc, :] = jnp.dot(v_h, w_xo_ref[hc, :],
                                     preferred_element_type=jnp.float32).astype(bf16)

    halves = [slice(i * ATTN_HALF, (i + 1) * ATTN_HALF) for i in range(tm // ATTN_HALF)]
    hs = [_rms(x_ref[r, :], gx_ref[...]).astype(bf16) for r in halves]
    scores = [jnp.dot(h, wqk_ref[...], preferred_element_type=jnp.float32) for h in hs]
    for hh in range(X_HEADS):
        mc = slice(hh * m_len, (hh + 1) * m_len)
        for s, r in zip(scores, halves):
            p = jnp.exp(s[:, mc] - jnp.max(s[:, mc], axis=-1, keepdims=True))
            l = jnp.sum(p, axis=-1, keepdims=True)
            p_ref[r, mc] = (p * (1.0 / l)).astype(bf16)
    for r in halves:
        y = x_ref[r, :] + jnp.dot(p_ref[r, :], vxo_ref[...],
                                  preferred_element_type=jnp.float32)
        o_ref[r, :] = _rms(y, gf_ref[...])


def _stage(cols):
    rows = 1 << int(math.log2(STAGE_BYTES // (4 * cols)))
    return pltpu.VMEM((N_STAGE, rows, cols), jnp.float32)


def _resident(shape):
    return pl.BlockSpec(shape, lambda *_: (0,) * len(shape),
                        pipeline_mode=pl.Buffered(1))


def _resident_layer(shape):
    return pl.BlockSpec((None,) + tuple(shape[1:]), lambda *_: (0,) * len(shape),
                        pipeline_mode=pl.Buffered(1))


def kernel(x, mem, norm_mix_g, w_in, conv_w, gm_ln_g, gm_ln_b, gm_ws, gm_bs, w_out,
           norm_x_g, norm_mem_g, w_q, w_kv, w_xo, norm_final_g):
    b, s, d = x.shape
    m_len = mem.shape[1]
    assert w_in.shape[0] == 1, "the final norm is fused into the (single) layer's attention call"
    assert s % TM == 0 and TM % CHUNK == 0 and d % CW == 0
    assert s % TM_ATTN == 0 and TM_ATTN % ATTN_HALF == 0
    bf16 = jnp.bfloat16
    f32 = jnp.float32
    params = pltpu.CompilerParams(
        dimension_semantics=("arbitrary", "arbitrary"), vmem_limit_bytes=VMEM_LIMIT)
    row = lambda a: a.reshape(1, -1)
    hbm = pl.BlockSpec(memory_space=pl.ANY)
    tile = pl.BlockSpec((None, TM, d), lambda i, t: (i, t, 0))
    sems = pltpu.SemaphoreType.DMA((N_STAGE,))

    x = pl.pallas_call(
        _mixer_kernel,
        grid=(b, s // TM),
        in_specs=[tile,
                  _resident((1, d)),
                  hbm,
                  hbm,
                  _resident((1, d)), _resident((1, d)),
                  _resident(gm_ws.shape[1:]),
                  _resident_layer(gm_bs.shape),
                  hbm],
        out_specs=tile,
        out_shape=jax.ShapeDtypeStruct((b, s, d), f32),
        scratch_shapes=[pltpu.VMEM((d, N_SECT * d), bf16),
                        pltpu.VMEM((2 * d, d), bf16),
                        _stage(N_SECT * d), _stage(d), sems,
                        pltpu.VMEM((TM, d), bf16),
                        pltpu.VMEM((2, TM, N_SECT * CW), f32),
                        pltpu.VMEM((SUBLANES, d), f32),
                        pltpu.VMEM((TM, 2 * d), bf16),
                        pltpu.VMEM((CHUNK, gm_bs.shape[1]), f32),
                        pltpu.VMEM(conv_w.shape[1:], f32),
                        pltpu.SemaphoreType.DMA(())],
        compiler_params=params,
        name="mixer",
    )(x, row(norm_mix_g[0]), w_in[0], conv_w, row(gm_ln_g[0]), row(gm_ln_b[0]),
      gm_ws[0], gm_bs, w_out[0])

    attn_tile = pl.BlockSpec((None, TM_ATTN, d), lambda i, t: (i, t, 0))
    return pl.pallas_call(
        _attn_kernel,
        grid=(b, s // TM_ATTN),
        in_specs=[attn_tile,
                  pl.BlockSpec((None, m_len, d), lambda i, t: (i, 0, 0)),
                  _resident((1, d)), _resident((1, d)),
                  hbm, hbm, hbm,
                  _resident((1, d))],
        out_specs=attn_tile,
        out_shape=jax.ShapeDtypeStruct((b, s, d), f32),
        scratch_shapes=[pltpu.VMEM((d, d), bf16),
                        pltpu.VMEM((d, 2 * d), bf16),
                        pltpu.VMEM((d, d), bf16),
                        _stage(d), _stage(2 * d), sems,
                        pltpu.VMEM((d, X_HEADS * m_len), bf16),
                        pltpu.VMEM((X_HEADS * m_len, d), bf16),
                        pltpu.VMEM((TM_ATTN, X_HEADS * m_len), bf16)],
        compiler_params=params,
        name="xattn",
    )(x, mem, row(norm_x_g[0]), row(norm_mem_g[0]), w_q[0], w_kv[0], w_xo[0],
      row(norm_final_g))
```

```python
import math

import jax
import jax.numpy as jnp
from jax import lax
from jax.experimental import pallas as pl
from jax.experimental.pallas import tpu as pltpu

EPS = 1e-6
CONV_K = 3
CHUNK = 128
HEAD_DIM = 128
N_SECT = 7
X_HEADS = 4
SUBLANES = 8

TM = 512
TM_ATTN = 1024
ATTN_HALF = 512
CW = 256
N_STAGE = 3
STAGE_BYTES = 1 << 21
VMEM_LIMIT = 56 * 1024 * 1024


def _rms(x, g):
    ms = jnp.mean(x * x, axis=-1, keepdims=True)
    return x * lax.rsqrt(ms + EPS) * g


def _sigmoid(z):
    return 0.5 * (1.0 + jnp.tanh(0.5 * z))


def _gelu_tanh(x):
    c = math.sqrt(2.0 / math.pi)
    return 0.5 * x * (1.0 + jnp.tanh(c * (x + 0.044715 * (x * x * x))))


def _load_weights_bf16(jobs, sem_ref):
    blocks = [(src, dst, stage, r)
              for src, dst, stage in jobs
              for r in range(0, src.shape[0], stage.shape[1])]

    def copy(k):
        src, _, stage, r = blocks[k]
        return pltpu.make_async_copy(src.at[pl.ds(r, stage.shape[1])],
                                     stage.at[k % N_STAGE], sem_ref.at[k % N_STAGE])

    for k in range(min(N_STAGE - 1, len(blocks))):
        copy(k).start()
    for k, (_, dst, stage, r) in enumerate(blocks):
        if k + N_STAGE - 1 < len(blocks):
            copy(k + N_STAGE - 1).start()
        copy(k).wait()
        dst[r:r + stage.shape[1], :] = stage[k % N_STAGE].astype(jnp.bfloat16)


def _first_grid_step():
    return (pl.program_id(0) == 0) & (pl.program_id(1) == 0)


def _mixer_kernel(x_ref, g_ref, w_in_hbm, conv_w_hbm, ln_g_ref, ln_b_ref,
                  ws_ref, bs_ref, w_out_hbm, o_ref,
                  w_in_ref, w_out_ref, stage_in_ref, stage_out_ref, sem_ref,
                  h_ref, proj_ref, carry_ref, mix_ref, bst_ref, conv_w_ref, conv_sem):
    tm, d = x_ref.shape
    n_col_chunks = d // CW
    n_row_blocks = tm // CHUNK
    heads_per_chunk = CW // HEAD_DIM

    @pl.when(_first_grid_step())
    def _():
        conv_copy = pltpu.make_async_copy(conv_w_hbm, conv_w_ref, conv_sem)
        conv_copy.start()
        _load_weights_bf16([(w_in_hbm, w_in_ref, stage_in_ref),
                            (w_out_hbm, w_out_ref, stage_out_ref)], sem_ref)
        conv_copy.wait()

    @pl.when(pl.program_id(1) == 0)
    def _():
        carry_ref[...] = jnp.zeros_like(carry_ref)

    h_ref[...] = _rms(x_ref[...], g_ref[...]).astype(jnp.bfloat16)
    bst_ref[...] = bs_ref[...].T

    tri = (lax.broadcasted_iota(jnp.int32, (CHUNK, CHUNK), 0)
           >= lax.broadcasted_iota(jnp.int32, (CHUNK, CHUNK), 1))

    def in_proj(j):
        for s in range(N_SECT):
            proj_ref[j % 2, :, s * CW:(s + 1) * CW] = jnp.dot(
                h_ref[...], w_in_ref[:, s * d + j * CW:s * d + (j + 1) * CW],
                preferred_element_type=jnp.float32)

    def out_proj(c0, c1):
        part = [jnp.dot(mix_ref[:, o + c0:o + c1], w_out_ref[o + c0:o + c1, :],
                        preferred_element_type=jnp.float32) for o in (0, d)]
        return part[0] + part[1]

    in_proj(0)
    for j in range(n_col_chunks):
        cols = slice(j * CW, (j + 1) * CW)
        if j + 1 < n_col_chunks:
            in_proj(j + 1)
        else:
            o_ref[...] = x_ref[...] + out_proj(0, j * CW)

        def sect(s, rows, j=j):
            return proj_ref[j % 2, rows, s * CW:(s + 1) * CW]

        w0 = conv_w_ref[0, :, cols]
        w1 = conv_w_ref[1, :, cols]
        w2 = conv_w_ref[2, :, cols]
        wc = [jnp.where(tri, ws_ref[j * heads_per_chunk + hh], 0.0).astype(jnp.bfloat16)
              for hh in range(heads_per_chunk)]

        for r in range(n_row_blocks):
            rows = slice(r * CHUNK, (r + 1) * CHUNK)
            g = sect(1, rows) * sect(2, rows)
            if r == 0:
                prev = carry_ref[:, cols]
            else:
                prows = slice(r * CHUNK - SUBLANES, r * CHUNK)
                prev = sect(1, prows) * sect(2, prows)
            if r == n_row_blocks - 1:
                carry_ref[:, cols] = g[CHUNK - SUBLANES:, :]
            ext = jnp.concatenate([prev, g], axis=0)
            g1 = ext[SUBLANES - 1:SUBLANES - 1 + CHUNK, :]
            g2 = ext[SUBLANES - 2:SUBLANES - 2 + CHUNK, :]
            conv = w0 * g2 + w1 * g1 + w2 * g
            za = sect(3, rows)
            a = sect(0, rows) * conv * (za * _sigmoid(za))
            mix_ref[rows, cols] = a.astype(jnp.bfloat16)

            u = _gelu_tanh(sect(4, rows))
            v = _gelu_tanh(sect(5, rows))
            zb = sect(6, rows)
            sps = []
            for hh in range(heads_per_chunk):
                hc = slice(hh * HEAD_DIM, (hh + 1) * HEAD_DIM)
                gcols = slice(j * CW + hh * HEAD_DIM, j * CW + (hh + 1) * HEAD_DIM)
                vh = v[:, hc]
                mu = jnp.mean(vh, axis=-1, keepdims=True)
                vc = vh - mu
                var = jnp.mean(vc * vc, axis=-1, keepdims=True)
                vn = vc * lax.rsqrt(var + EPS) * ln_g_ref[:, gcols] + ln_b_ref[:, gcols]
                sp = jnp.dot(wc[hh], vn.astype(jnp.bfloat16),
                             preferred_element_type=jnp.float32)
                head = j * heads_per_chunk + hh
                sps.append(sp + bst_ref[:, head:head + 1])
            sp = jnp.concatenate(sps, axis=-1)
            bo = u * sp * (zb * _sigmoid(zb))
            mix_ref[rows, d + j * CW:d + (j + 1) * CW] = bo.astype(jnp.bfloat16)

    o_ref[...] += out_proj((n_col_chunks - 1) * CW, d)


def _attn_kernel(x_ref, mem_ref, gx_ref, gm_ref, w_q_hbm, w_kv_hbm, w_xo_hbm, gf_ref,
                 o_ref,
                 w_q_ref, w_kv_ref, w_xo_ref, stage_d_ref, stage_kv_ref, sem_ref,
                 wqk_ref, vxo_ref, p_ref):
    tm, d = x_ref.shape
    m_len = mem_ref.shape[0]
    hd = d // X_HEADS
    bf16 = jnp.bfloat16

    @pl.when(_first_grid_step())
    def _():
        _load_weights_bf16([(w_q_hbm, w_q_ref, stage_d_ref),
                            (w_kv_hbm, w_kv_ref, stage_kv_ref),
                            (w_xo_hbm, w_xo_ref, stage_d_ref)], sem_ref)

    @pl.when(pl.program_id(1) == 0)
    def _():
        m = _rms(mem_ref[...], gm_ref[...]).astype(bf16)
        kv = jnp.dot(m, w_kv_ref[...], preferred_element_type=jnp.float32)
        for hh in range(X_HEADS):
            hc = slice(hh * hd, (hh + 1) * hd)
            mc = slice(hh * m_len, (hh + 1) * m_len)
            k_h = kv[:, hc].astype(bf16)
            v_h = kv[:, d + hh * hd:d + (hh + 1) * hd].astype(bf16)
            wqk = lax.dot_general(w_q_ref[:, hc], k_h, (((1,), (1,)), ((), ())),
                                  preferred_element_type=jnp.float32)
            wqk_ref[:, mc] = (wqk * (1.0 / math.sqrt(hd))).astype(bf16)
            vxo_ref[mc, :] = jnp.dot(v_h, w_xo_ref[hc, :],
                                     preferred_element_type=jnp.float32).astype(bf16)

    halves = [slice(i * ATTN_HALF, (i + 1) * ATTN_HALF) for i in range(tm // ATTN_HALF)]
    hs = [_rms(x_ref[r, :], gx_ref[...]).astype(bf16) for r in halves]
    scores = [jnp.dot(h, wqk_ref[...], preferred_element_type=jnp.float32) for h in hs]
    for hh in range(X_HEADS):
        mc = slice(hh * m_len, (hh + 1) * m_len)
        for s, r in zip(scores, halves):
            p = jnp.exp(s[:, mc] - jnp.max(s[:, mc], axis=-1, keepdims=True))
            l = jnp.sum(p, axis=-1, keepdims=True)
            p_ref[r, mc] = (p * (1.0 / l)).astype(bf16)
    for r in halves:
        y = x_ref[r, :] + jnp.dot(p_ref[r, :], vxo_ref[...],
                                  preferred_element_type=jnp.float32)
        o_ref[r, :] = _rms(y, gf_ref[...])


def _stage(cols):
    rows = 1 << int(math.log2(STAGE_BYTES // (4 * cols)))
    return pltpu.VMEM((N_STAGE, rows, cols), jnp.float32)


def _resident(shape):
    return pl.BlockSpec(shape, lambda *_: (0,) * len(shape),
                        pipeline_mode=pl.Buffered(1))


def _resident_layer(shape):
    return pl.BlockSpec((None,) + tuple(shape[1:]), lambda *_: (0,) * len(shape),
                        pipeline_mode=pl.Buffered(1))


def kernel(x, mem, norm_mix_g, w_in, conv_w, gm_ln_g, gm_ln_b, gm_ws, gm_bs, w_out,
           norm_x_g, norm_mem_g, w_q, w_kv, w_xo, norm_final_g):
    b, s, d = x.shape
    m_len = mem.shape[1]
    assert w_in.shape[0] == 1, "the final norm is fused into the (single) layer's attention call"
    assert s % TM == 0 and TM % CHUNK == 0 and d % CW == 0
    assert s % TM_ATTN == 0 and TM_ATTN % ATTN_HALF == 0
    bf16 = jnp.bfloat16
    f32 = jnp.float32
    params = pltpu.CompilerParams(
        dimension_semantics=("arbitrary", "arbitrary"), vmem_limit_bytes=VMEM_LIMIT)
    row = lambda a: a.reshape(1, -1)
    hbm = pl.BlockSpec(memory_space=pl.ANY)
    tile = pl.BlockSpec((None, TM, d), lambda i, t: (i, t, 0))
    sems = pltpu.SemaphoreType.DMA((N_STAGE,))

    x = pl.pallas_call(
        _mixer_kernel,
        grid=(b, s // TM),
        in_specs=[tile,
                  _resident((1, d)),
                  hbm,
                  hbm,
                  _resident((1, d)), _resident((1, d)),
                  _resident(gm_ws.shape[1:]),
                  _resident_layer(gm_bs.shape),
                  hbm],
        out_specs=tile,
        out_shape=jax.ShapeDtypeStruct((b, s, d), f32),
        scratch_shapes=[pltpu.VMEM((d, N_SECT * d), bf16),
                        pltpu.VMEM((2 * d, d), bf16),
                        _stage(N_SECT * d), _stage(d), sems,
                        pltpu.VMEM((TM, d), bf16),
                        pltpu.VMEM((2, TM, N_SECT * CW), f32),
                        pltpu.VMEM((SUBLANES, d), f32),
                        pltpu.VMEM((TM, 2 * d), bf16),
                        pltpu.VMEM((CHUNK, gm_bs.shape[1]), f32),
                        pltpu.VMEM((CONV_K, 1, d), f32),
                        pltpu.SemaphoreType.DMA(())],
        compiler_params=params,
        name="mixer",
    )(x, row(norm_mix_g[0]), w_in[0], conv_w.reshape(CONV_K, 1, d), row(gm_ln_g[0]), row(gm_ln_b[0]),
      gm_ws[0], gm_bs, w_out[0])

    attn_tile = pl.BlockSpec((None, TM_ATTN, d), lambda i, t: (i, t, 0))
    return pl.pallas_call(
        _attn_kernel,
        grid=(b, s // TM_ATTN),
        in_specs=[attn_tile,
                  pl.BlockSpec((None, m_len, d), lambda i, t: (i, 0, 0)),
                  _resident((1, d)), _resident((1, d)),
                  hbm, hbm, hbm,
                  _resident((1, d))],
        out_specs=attn_tile,
        out_shape=jax.ShapeDtypeStruct((b, s, d), f32),
        scratch_shapes=[pltpu.VMEM((d, d), bf16),
                        pltpu.VMEM((d, 2 * d), bf16),
                        pltpu.VMEM((d, d), bf16),
                        _stage(d), _stage(2 * d), sems,
                        pltpu.VMEM((d, X_HEADS * m_len), bf16),
                        pltpu.VMEM((X_HEADS * m_len, d), bf16),
                        pltpu.VMEM((TM_ATTN, X_HEADS * m_len), bf16)],
        compiler_params=params,
        name="xattn",
    )(x, mem, row(norm_x_g[0]), row(norm_mem_g[0]), w_q[0], w_kv[0], w_xo[0],
      row(norm_final_g))
```

```python
import math

import jax
import jax.numpy as jnp
from jax import lax
from jax.experimental import pallas as pl
from jax.experimental.pallas import tpu as pltpu

EPS = 1e-6
CONV_K = 3
CHUNK = 128
HEAD_DIM = 128
N_SECT = 7
X_HEADS = 4
SUBLANES = 8

TM = 512
TM_ATTN = 1024
ATTN_HALF = 512
CW = 256
N_STAGE = 3
STAGE_BYTES = 1 << 21
VMEM_LIMIT = 56 * 1024 * 1024
PARAM_ROWS = 4


def _rms(x, g):
    ms = jnp.mean(x * x, axis=-1, keepdims=True)
    return x * lax.rsqrt(ms + EPS) * g


def _sigmoid(z):
    return 0.5 * (1.0 + jnp.tanh(0.5 * z))


def _gelu_tanh(x):
    c = math.sqrt(2.0 / math.pi)
    return 0.5 * x * (1.0 + jnp.tanh(c * (x + 0.044715 * (x * x * x))))


def _load_weights_bf16(jobs, sem_ref):
    blocks = [(src, dst, stage, r)
              for src, dst, stage in jobs
              for r in range(0, src.shape[0], stage.shape[1])]

    def copy(k):
        src, _, stage, r = blocks[k]
        return pltpu.make_async_copy(src.at[pl.ds(r, stage.shape[1])],
                                     stage.at[k % N_STAGE], sem_ref.at[k % N_STAGE])

    for k in range(min(N_STAGE - 1, len(blocks))):
        copy(k).start()
    for k, (_, dst, stage, r) in enumerate(blocks):
        if k + N_STAGE - 1 < len(blocks):
            copy(k + N_STAGE - 1).start()
        copy(k).wait()
        dst[r:r + stage.shape[1], :] = stage[k % N_STAGE].astype(jnp.bfloat16)


def _row_copies(srcs, dst_ref, sem_ref):
    copies, r = [], 0
    for i, src in enumerate(srcs):
        copies.append(pltpu.make_async_copy(src, dst_ref.at[pl.ds(r, src.shape[0])],
                                            sem_ref.at[i]))
        r += src.shape[0]
    return copies


def _first_grid_step():
    return (pl.program_id(0) == 0) & (pl.program_id(1) == 0)


def _mixer_kernel(x_ref, g_ref, w_in_hbm, conv_w_hbm, ln_g_ref, ln_b_ref,
                  ws_ref, bs_ref, w_out_hbm, o_ref,
                  w_in_ref, w_out_ref, stage_in_ref, stage_out_ref, sem_ref,
                  h_ref, proj_ref, carry_ref, mix_ref, bst_ref, conv_w_ref, conv_sem):
    tm, d = x_ref.shape
    n_col_chunks = d // CW
    n_row_blocks = tm // CHUNK
    heads_per_chunk = CW // HEAD_DIM

    @pl.when(_first_grid_step())
    def _():
        small = _row_copies([conv_w_hbm], conv_w_ref, conv_sem)
        for c in small:
            c.start()
        _load_weights_bf16([(w_in_hbm, w_in_ref, stage_in_ref),
                            (w_out_hbm, w_out_ref, stage_out_ref)], sem_ref)
        for c in small:
            c.wait()

    @pl.when(pl.program_id(1) == 0)
    def _():
        carry_ref[...] = jnp.zeros_like(carry_ref)

    h_ref[...] = _rms(x_ref[...], g_ref[...]).astype(jnp.bfloat16)
    bst_ref[...] = bs_ref[...].T

    tri = (lax.broadcasted_iota(jnp.int32, (CHUNK, CHUNK), 0)
           >= lax.broadcasted_iota(jnp.int32, (CHUNK, CHUNK), 1))

    def in_proj(j):
        for s in range(N_SECT):
            proj_ref[j % 2, :, s * CW:(s + 1) * CW] = jnp.dot(
                h_ref[...], w_in_ref[:, s * d + j * CW:s * d + (j + 1) * CW],
                preferred_element_type=jnp.float32)

    def out_proj(c0, c1):
        part = [jnp.dot(mix_ref[:, o + c0:o + c1], w_out_ref[o + c0:o + c1, :],
                        preferred_element_type=jnp.float32) for o in (0, d)]
        return part[0] + part[1]

    in_proj(0)
    for j in range(n_col_chunks):
        cols = slice(j * CW, (j + 1) * CW)
        if j + 1 < n_col_chunks:
            in_proj(j + 1)
        else:
            o_ref[...] = x_ref[...] + out_proj(0, j * CW)

        def sect(s, rows, j=j):
            return proj_ref[j % 2, rows, s * CW:(s + 1) * CW]

        w0 = conv_w_ref[0, :, cols]
        w1 = conv_w_ref[1, :, cols]
        w2 = conv_w_ref[2, :, cols]
        wc = [jnp.where(tri, ws_ref[j * heads_per_chunk + hh], 0.0).astype(jnp.bfloat16)
              for hh in range(heads_per_chunk)]

        for r in range(n_row_blocks):
            rows = slice(r * CHUNK, (r + 1) * CHUNK)
            g = sect(1, rows) * sect(2, rows)
            if r == 0:
                prev = carry_ref[:, cols]
            else:
                prows = slice(r * CHUNK - SUBLANES, r * CHUNK)
                prev = sect(1, prows) * sect(2, prows)
            if r == n_row_blocks - 1:
                carry_ref[:, cols] = g[CHUNK - SUBLANES:, :]
            ext = jnp.concatenate([prev, g], axis=0)
            g1 = ext[SUBLANES - 1:SUBLANES - 1 + CHUNK, :]
            g2 = ext[SUBLANES - 2:SUBLANES - 2 + CHUNK, :]
            conv = w0 * g2 + w1 * g1 + w2 * g
            za = sect(3, rows)
            a = sect(0, rows) * conv * (za * _sigmoid(za))
            mix_ref[rows, cols] = a.astype(jnp.bfloat16)

            u = _gelu_tanh(sect(4, rows))
            v = _gelu_tanh(sect(5, rows))
            zb = sect(6, rows)
            sps = []
            for hh in range(heads_per_chunk):
                hc = slice(hh * HEAD_DIM, (hh + 1) * HEAD_DIM)
                gcols = slice(j * CW + hh * HEAD_DIM, j * CW + (hh + 1) * HEAD_DIM)
                vh = v[:, hc]
                mu = jnp.mean(vh, axis=-1, keepdims=True)
                vc = vh - mu
                var = jnp.mean(vc * vc, axis=-1, keepdims=True)
                vn = vc * lax.rsqrt(var + EPS) * ln_g_ref[:, gcols] + ln_b_ref[:, gcols]
                sp = jnp.dot(wc[hh], vn.astype(jnp.bfloat16),
                             preferred_element_type=jnp.float32)
                head = j * heads_per_chunk + hh
                sps.append(sp + bst_ref[:, head:head + 1])
            sp = jnp.concatenate(sps, axis=-1)
            bo = u * sp * (zb * _sigmoid(zb))
            mix_ref[rows, d + j * CW:d + (j + 1) * CW] = bo.astype(jnp.bfloat16)

    o_ref[...] += out_proj((n_col_chunks - 1) * CW, d)


def _attn_kernel(x_ref, mem_ref, gx_hbm, gm_hbm, w_q_hbm, w_kv_hbm, w_xo_hbm, gf_hbm,
                 o_ref,
                 w_q_ref, w_kv_ref, w_xo_ref, stage_d_ref, stage_kv_ref, sem_ref,
                 wqk_ref, vxo_ref, p_ref, gain_ref, gain_sem):
    tm, d = x_ref.shape
    m_len = mem_ref.shape[0]
    hd = d // X_HEADS
    bf16 = jnp.bfloat16

    @pl.when(_first_grid_step())
    def _():
        small = _row_copies([gx_hbm, gm_hbm, gf_hbm], gain_ref, gain_sem)
        for c in small:
            c.start()
        _load_weights_bf16([(w_q_hbm, w_q_ref, stage_d_ref),
                            (w_kv_hbm, w_kv_ref, stage_kv_ref),
                            (w_xo_hbm, w_xo_ref, stage_d_ref)], sem_ref)
        for c in small:
            c.wait()

    @pl.when(pl.program_id(1) == 0)
    def _():
        m = _rms(mem_ref[...], gain_ref[1]).astype(bf16)
        kv = jnp.dot(m, w_kv_ref[...], preferred_element_type=jnp.float32)
        for hh in range(X_HEADS):
            hc = slice(hh * hd, (hh + 1) * hd)
            mc = slice(hh * m_len, (hh + 1) * m_len)
            k_h = kv[:, hc].astype(bf16)
            v_h = kv[:, d + hh * hd:d + (hh + 1) * hd].astype(bf16)
            wqk = lax.dot_general(w_q_ref[:, hc], k_h, (((1,), (1,)), ((), ())),
                                  preferred_element_type=jnp.float32)
            wqk_ref[:, mc] = (wqk * (1.0 / math.sqrt(hd))).astype(bf16)
            vxo_ref[mc, :] = jnp.dot(v_h, w_xo_ref[hc, :],
                                     preferred_element_type=jnp.float32).astype(bf16)

    halves = [slice(i * ATTN_HALF, (i + 1) * ATTN_HALF) for i in range(tm // ATTN_HALF)]
    hs = [_rms(x_ref[r, :], gain_ref[0]).astype(bf16) for r in halves]
    scores = [jnp.dot(h, wqk_ref[...], preferred_element_type=jnp.float32) for h in hs]
    for hh in range(X_HEADS):
        mc = slice(hh * m_len, (hh + 1) * m_len)
        for s, r in zip(scores, halves):
            p = jnp.exp(s[:, mc] - jnp.max(s[:, mc], axis=-1, keepdims=True))
            l = jnp.sum(p, axis=-1, keepdims=True)
            p_ref[r, mc] = (p * (1.0 / l)).astype(bf16)
    for r in halves:
        y = x_ref[r, :] + jnp.dot(p_ref[r, :], vxo_ref[...],
                                  preferred_element_type=jnp.float32)
        o_ref[r, :] = _rms(y, gain_ref[2])


def _stage(cols):
    rows = 1 << int(math.log2(STAGE_BYTES // (4 * cols)))
    return pltpu.VMEM((N_STAGE, rows, cols), jnp.float32)


def _resident(shape):
    return pl.BlockSpec(shape, lambda *_: (0,) * len(shape),
                        pipeline_mode=pl.Buffered(1))


def _resident_layer(shape):
    return pl.BlockSpec((None,) + tuple(shape[1:]), lambda *_: (0,) * len(shape),
                        pipeline_mode=pl.Buffered(1))


def kernel(x, mem, norm_mix_g, w_in, conv_w, gm_ln_g, gm_ln_b, gm_ws, gm_bs, w_out,
           norm_x_g, norm_mem_g, w_q, w_kv, w_xo, norm_final_g):
    b, s, d = x.shape
    m_len = mem.shape[1]
    assert w_in.shape[0] == 1, "the final norm is fused into the (single) layer's attention call"
    assert s % TM == 0 and TM % CHUNK == 0 and d % CW == 0
    assert s % TM_ATTN == 0 and TM_ATTN % ATTN_HALF == 0
    bf16 = jnp.bfloat16
    f32 = jnp.float32
    params = pltpu.CompilerParams(
        dimension_semantics=("arbitrary", "arbitrary"), vmem_limit_bytes=VMEM_LIMIT)
    row = lambda a: a.reshape(1, -1)
    gain = lambda a: a.reshape(1, 1, d)
    hbm = pl.BlockSpec(memory_space=pl.ANY)
    tile = pl.BlockSpec((None, TM, d), lambda i, t: (i, t, 0))
    sems = pltpu.SemaphoreType.DMA((N_STAGE,))

    x = pl.pallas_call(
        _mixer_kernel,
        grid=(b, s // TM),
        in_specs=[tile,
                  _resident((1, d)),
                  hbm,
                  hbm,
                  _resident((1, d)), _resident((1, d)),
                  _resident(gm_ws.shape[1:]),
                  _resident_layer(gm_bs.shape),
                  hbm],
        out_specs=tile,
        out_shape=jax.ShapeDtypeStruct((b, s, d), f32),
        scratch_shapes=[pltpu.VMEM((d, N_SECT * d), bf16),
                        pltpu.VMEM((2 * d, d), bf16),
                        _stage(N_SECT * d), _stage(d), sems,
                        pltpu.VMEM((TM, d), bf16),
                        pltpu.VMEM((2, TM, N_SECT * CW), f32),
                        pltpu.VMEM((SUBLANES, d), f32),
                        pltpu.VMEM((TM, 2 * d), bf16),
                        pltpu.VMEM((CHUNK, gm_bs.shape[1]), f32),
                        pltpu.VMEM((PARAM_ROWS, 1, d), f32),
                        pltpu.SemaphoreType.DMA((1,))],
        compiler_params=params,
        name="mixer",
    )(x, row(norm_mix_g[0]), w_in[0], conv_w.reshape(CONV_K, 1, d), row(gm_ln_g[0]), row(gm_ln_b[0]),
      gm_ws[0], gm_bs, w_out[0])

    attn_tile = pl.BlockSpec((None, TM_ATTN, d), lambda i, t: (i, t, 0))
    return pl.pallas_call(
        _attn_kernel,
        grid=(b, s // TM_ATTN),
        in_specs=[attn_tile,
                  pl.BlockSpec((None, m_len, d), lambda i, t: (i, 0, 0)),
                  hbm, hbm,
                  hbm, hbm, hbm,
                  hbm],
        out_specs=attn_tile,
        out_shape=jax.ShapeDtypeStruct((b, s, d), f32),
        scratch_shapes=[pltpu.VMEM((d, d), bf16),
                        pltpu.VMEM((d, 2 * d), bf16),
                        pltpu.VMEM((d, d), bf16),
                        _stage(d), _stage(2 * d), sems,
                        pltpu.VMEM((d, X_HEADS * m_len), bf16),
                        pltpu.VMEM((X_HEADS * m_len, d), bf16),
                        pltpu.VMEM((TM_ATTN, X_HEADS * m_len), bf16),
                        pltpu.VMEM((PARAM_ROWS, 1, d), f32),
                        pltpu.SemaphoreType.DMA((3,))],
        compiler_params=params,
        name="xattn",
    )(x, mem, gain(norm_x_g), gain(norm_mem_g), w_q[0], w_kv[0], w_xo[0],
      gain(norm_final_g))
```

```python
import math

import jax
import jax.numpy as jnp
from jax import lax
from jax.experimental import pallas as pl
from jax.experimental.pallas import tpu as pltpu

EPS = 1e-6
CONV_K = 3
CHUNK = 128
HEAD_DIM = 128
N_SECT = 7
X_HEADS = 4
SUBLANES = 8

TM = 1024
TM_ATTN = 1024
ATTN_HALF = 512
CW = 256
N_STAGE = 2
STAGE_BYTES = 1 << 21
VMEM_LIMIT = 60 * 1024 * 1024


def _rms(x, g):
    ms = jnp.mean(x * x, axis=-1, keepdims=True)
    return x * lax.rsqrt(ms + EPS) * g


def _sigmoid(z):
    return 0.5 * (1.0 + jnp.tanh(0.5 * z))


def _gelu_tanh(x):
    c = math.sqrt(2.0 / math.pi)
    return 0.5 * x * (1.0 + jnp.tanh(c * (x + 0.044715 * (x * x * x))))


def _load_weights_bf16(jobs, sem_ref):
    blocks = [(src, dst, stage, r)
              for src, dst, stage in jobs
              for r in range(0, src.shape[0], stage.shape[1])]

    def copy(k):
        src, _, stage, r = blocks[k]
        return pltpu.make_async_copy(src.at[pl.ds(r, stage.shape[1])],
                                     stage.at[k % N_STAGE], sem_ref.at[k % N_STAGE])

    for k in range(min(N_STAGE - 1, len(blocks))):
        copy(k).start()
    for k, (_, dst, stage, r) in enumerate(blocks):
        if k + N_STAGE - 1 < len(blocks):
            copy(k + N_STAGE - 1).start()
        copy(k).wait()
        dst[r:r + stage.shape[1], :] = stage[k % N_STAGE].astype(jnp.bfloat16)


def _first_grid_step():
    return (pl.program_id(0) == 0) & (pl.program_id(1) == 0)


def _mixer_kernel(x_ref, g_ref, w_in_hbm, conv_w_hbm, ln_g_ref, ln_b_ref,
                  ws_ref, bs_ref, w_out_hbm, o_ref,
                  w_in_ref, w_out_ref, stage_in_ref, stage_out_ref, sem_ref,
                  h_ref, proj_ref, carry_ref, mix_ref, bst_ref, conv_w_ref, conv_sem):
    tm, d = x_ref.shape
    n_col_chunks = d // CW
    n_row_blocks = tm // CHUNK
    heads_per_chunk = CW // HEAD_DIM

    @pl.when(_first_grid_step())
    def _():
        conv_copy = pltpu.make_async_copy(conv_w_hbm.at[0], conv_w_ref, conv_sem)
        conv_copy.start()
        _load_weights_bf16([(w_in_hbm, w_in_ref, stage_in_ref),
                            (w_out_hbm, w_out_ref, stage_out_ref)], sem_ref)
        conv_copy.wait()

    @pl.when(pl.program_id(1) == 0)
    def _():
        carry_ref[...] = jnp.zeros_like(carry_ref)

    h_ref[...] = _rms(x_ref[...], g_ref[...]).astype(jnp.bfloat16)
    bst_ref[...] = bs_ref[...].T

    tri = (lax.broadcasted_iota(jnp.int32, (CHUNK, CHUNK), 0)
           >= lax.broadcasted_iota(jnp.int32, (CHUNK, CHUNK), 1))

    def in_proj(j):
        for s in range(N_SECT):
            proj_ref[j % 2, :, s * CW:(s + 1) * CW] = jnp.dot(
                h_ref[...], w_in_ref[:, s * d + j * CW:s * d + (j + 1) * CW],
                preferred_element_type=jnp.float32)

    def out_proj(c0, c1):
        part = [jnp.dot(mix_ref[:, o + c0:o + c1], w_out_ref[o + c0:o + c1, :],
                        preferred_element_type=jnp.float32) for o in (0, d)]
        return part[0] + part[1]

    in_proj(0)
    for j in range(n_col_chunks):
        cols = slice(j * CW, (j + 1) * CW)
        if j + 1 < n_col_chunks:
            in_proj(j + 1)
        else:
            o_ref[...] = x_ref[...] + out_proj(0, j * CW)

        def sect(s, rows, j=j):
            return proj_ref[j % 2, rows, s * CW:(s + 1) * CW]

        w0 = conv_w_ref[0:1, cols]
        w1 = conv_w_ref[1:2, cols]
        w2 = conv_w_ref[2:3, cols]
        wc = [jnp.where(tri, ws_ref[j * heads_per_chunk + hh], 0.0).astype(jnp.bfloat16)
              for hh in range(heads_per_chunk)]

        for r in range(n_row_blocks):
            rows = slice(r * CHUNK, (r + 1) * CHUNK)
            g = sect(1, rows) * sect(2, rows)
            if r == 0:
                prev = carry_ref[:, cols]
            else:
                prows = slice(r * CHUNK - SUBLANES, r * CHUNK)
                prev = sect(1, prows) * sect(2, prows)
            if r == n_row_blocks - 1:
                carry_ref[:, cols] = g[CHUNK - SUBLANES:, :]
            ext = jnp.concatenate([prev, g], axis=0)
            g1 = ext[SUBLANES - 1:SUBLANES - 1 + CHUNK, :]
            g2 = ext[SUBLANES - 2:SUBLANES - 2 + CHUNK, :]
            conv = w0 * g2 + w1 * g1 + w2 * g
            za = sect(3, rows)
            a = sect(0, rows) * conv * (za * _sigmoid(za))
            mix_ref[rows, cols] = a.astype(jnp.bfloat16)

            u = _gelu_tanh(sect(4, rows))
            v = _gelu_tanh(sect(5, rows))
            zb = sect(6, rows)
            sps = []
            for hh in range(heads_per_chunk):
                hc = slice(hh * HEAD_DIM, (hh + 1) * HEAD_DIM)
                gcols = slice(j * CW + hh * HEAD_DIM, j * CW + (hh + 1) * HEAD_DIM)
                vh = v[:, hc]
                mu = jnp.mean(vh, axis=-1, keepdims=True)
                vc = vh - mu
                var = jnp.mean(vc * vc, axis=-1, keepdims=True)
                vn = vc * lax.rsqrt(var + EPS) * ln_g_ref[:, gcols] + ln_b_ref[:, gcols]
                sp = jnp.dot(wc[hh], vn.astype(jnp.bfloat16),
                             preferred_element_type=jnp.float32)
                head = j * heads_per_chunk + hh
                sps.append(sp + bst_ref[:, head:head + 1])
            sp = jnp.concatenate(sps, axis=-1)
            bo = u * sp * (zb * _sigmoid(zb))
            mix_ref[rows, d + j * CW:d + (j + 1) * CW] = bo.astype(jnp.bfloat16)

    o_ref[...] += out_proj((n_col_chunks - 1) * CW, d)


def _attn_kernel(x_ref, mem_ref, gx_ref, gm_ref, w_q_hbm, w_kv_hbm, w_xo_hbm, gf_ref,
                 o_ref,
                 w_q_ref, w_kv_ref, w_xo_ref, stage_d_ref, stage_kv_ref, sem_ref,
                 wqk_ref, vxo_ref, p_ref):
    tm, d = x_ref.shape
    m_len = mem_ref.shape[0]
    hd = d // X_HEADS
    bf16 = jnp.bfloat16

    @pl.when(_first_grid_step())
    def _():
        _load_weights_bf16([(w_q_hbm, w_q_ref, stage_d_ref),
                            (w_kv_hbm, w_kv_ref, stage_kv_ref),
                            (w_xo_hbm, w_xo_ref, stage_d_ref)], sem_ref)

    @pl.when(pl.program_id(1) == 0)
    def _():
        m = _rms(mem_ref[...], gm_ref[...]).astype(bf16)
        kv = jnp.dot(m, w_kv_ref[...], preferred_element_type=jnp.float32)
        for hh in range(X_HEADS):
            hc = slice(hh * hd, (hh + 1) * hd)
            mc = slice(hh * m_len, (hh + 1) * m_len)
            k_h = kv[:, hc].astype(bf16)
            v_h = kv[:, d + hh * hd:d + (hh + 1) * hd].astype(bf16)
            wqk = lax.dot_general(w_q_ref[:, hc], k_h, (((1,), (1,)), ((), ())),
                                  preferred_element_type=jnp.float32)
            wqk_ref[:, mc] = (wqk * (1.0 / math.sqrt(hd))).astype(bf16)
            vxo_ref[mc, :] = jnp.dot(v_h, w_xo_ref[hc, :],
                                     preferred_element_type=jnp.float32).astype(bf16)

    halves = [slice(i * ATTN_HALF, (i + 1) * ATTN_HALF) for i in range(tm // ATTN_HALF)]
    hs = [_rms(x_ref[r, :], gx_ref[...]).astype(bf16) for r in halves]
    scores = [jnp.dot(h, wqk_ref[...], preferred_element_type=jnp.float32) for h in hs]
    for hh in range(X_HEADS):
        mc = slice(hh * m_len, (hh + 1) * m_len)
        for s, r in zip(scores, halves):
            p = jnp.exp(s[:, mc] - jnp.max(s[:, mc], axis=-1, keepdims=True))
            l = jnp.sum(p, axis=-1, keepdims=True)
            p_ref[r, mc] = (p * (1.0 / l)).astype(bf16)
    for r in halves:
        y = x_ref[r, :] + jnp.dot(p_ref[r, :], vxo_ref[...],
                                  preferred_element_type=jnp.float32)
        o_ref[r, :] = _rms(y, gf_ref[...])


def _stage(cols):
    rows = 1 << int(math.log2(STAGE_BYTES // (4 * cols)))
    return pltpu.VMEM((N_STAGE, rows, cols), jnp.float32)


def _resident(shape):
    return pl.BlockSpec(shape, lambda *_: (0,) * len(shape),
                        pipeline_mode=pl.Buffered(1))


def _resident_layer(shape):
    return pl.BlockSpec((None,) + tuple(shape[1:]), lambda *_: (0,) * len(shape),
                        pipeline_mode=pl.Buffered(1))


def kernel(x, mem, norm_mix_g, w_in, conv_w, gm_ln_g, gm_ln_b, gm_ws, gm_bs, w_out,
           norm_x_g, norm_mem_g, w_q, w_kv, w_xo, norm_final_g):
    b, s, d = x.shape
    m_len = mem.shape[1]
    assert w_in.shape[0] == 1, "the final norm is fused into the (single) layer's attention call"
    assert s % TM == 0 and TM % CHUNK == 0 and d % CW == 0
    assert s % TM_ATTN == 0 and TM_ATTN % ATTN_HALF == 0
    bf16 = jnp.bfloat16
    f32 = jnp.float32
    params = pltpu.CompilerParams(
        dimension_semantics=("arbitrary", "arbitrary"), vmem_limit_bytes=VMEM_LIMIT)
    row = lambda a: a.reshape(1, -1)
    hbm = pl.BlockSpec(memory_space=pl.ANY)
    tile = pl.BlockSpec((None, TM, d), lambda i, t: (i, t, 0))
    sems = pltpu.SemaphoreType.DMA((N_STAGE,))

    x = pl.pallas_call(
        _mixer_kernel,
        grid=(b, s // TM),
        in_specs=[tile,
                  _resident((1, d)),
                  hbm,
                  hbm,
                  _resident((1, d)), _resident((1, d)),
                  _resident(gm_ws.shape[1:]),
                  _resident_layer(gm_bs.shape),
                  hbm],
        out_specs=tile,
        out_shape=jax.ShapeDtypeStruct((b, s, d), f32),
        scratch_shapes=[pltpu.VMEM((d, N_SECT * d), bf16),
                        pltpu.VMEM((2 * d, d), bf16),
                        _stage(N_SECT * d), _stage(d), sems,
                        pltpu.VMEM((TM, d), bf16),
                        pltpu.VMEM((2, TM, N_SECT * CW), f32),
                        pltpu.VMEM((SUBLANES, d), f32),
                        pltpu.VMEM((TM, 2 * d), bf16),
                        pltpu.VMEM((CHUNK, gm_bs.shape[1]), f32),
                        pltpu.VMEM(conv_w.shape[1:], f32),
                        pltpu.SemaphoreType.DMA(())],
        compiler_params=params,
        name="mixer",
    )(x, row(norm_mix_g[0]), w_in[0], conv_w, row(gm_ln_g[0]), row(gm_ln_b[0]),
      gm_ws[0], gm_bs, w_out[0])

    attn_tile = pl.BlockSpec((None, TM_ATTN, d), lambda i, t: (i, t, 0))
    return pl.pallas_call(
        _attn_kernel,
        grid=(b, s // TM_ATTN),
        in_specs=[attn_tile,
                  pl.BlockSpec((None, m_len, d), lambda i, t: (i, 0, 0)),
                  _resident((1, d)), _resident((1, d)),
                  hbm, hbm, hbm,
                  _resident((1, d))],
        out_specs=attn_tile,
        out_shape=jax.ShapeDtypeStruct((b, s, d), f32),
        scratch_shapes=[pltpu.VMEM((d, d), bf16),
                        pltpu.VMEM((d, 2 * d), bf16),
                        pltpu.VMEM((d, d), bf16),
                        _stage(d), _stage(2 * d), sems,
                        pltpu.VMEM((d, X_HEADS * m_len), bf16),
                        pltpu.VMEM((X_HEADS * m_len, d), bf16),
                        pltpu.VMEM((TM_ATTN, X_HEADS * m_len), bf16)],
        compiler_params=params,
        name="xattn",
    )(x, mem, row(norm_x_g[0]), row(norm_mem_g[0]), w_q[0], w_kv[0], w_xo[0],
      row(norm_final_g))
```

```python
import math

import jax
import jax.numpy as jnp
from jax import lax
from jax.experimental import pallas as pl
from jax.experimental.pallas import tpu as pltpu

EPS = 1e-6
CONV_K = 3
CHUNK = 128
HEAD_DIM = 128
N_SECT = 7
X_HEADS = 4
SUBLANES = 8

TM = 1024
TM_ATTN = 1024
ATTN_HALF = 512
CW = 256
MIX_STAGES = 2
ATTN_STAGES = 3
STAGE_BYTES = 1 << 21
MIX_VMEM_LIMIT = 60 * 1024 * 1024
ATTN_VMEM_LIMIT = 56 * 1024 * 1024


def _rms(x, g):
    ms = jnp.mean(x * x, axis=-1, keepdims=True)
    return x * lax.rsqrt(ms + EPS) * g


def _sigmoid(z):
    return 0.5 * (1.0 + jnp.tanh(0.5 * z))


def _gelu_tanh(x):
    c = math.sqrt(2.0 / math.pi)
    return 0.5 * x * (1.0 + jnp.tanh(c * (x + 0.044715 * (x * x * x))))


def _load_weights_bf16(jobs, sem_ref):
    n_stage = sem_ref.shape[0]
    blocks = [(src, dst, stage, r)
              for src, dst, stage in jobs
              for r in range(0, src.shape[0], stage.shape[1])]

    def copy(k):
        src, _, stage, r = blocks[k]
        return pltpu.make_async_copy(src.at[pl.ds(r, stage.shape[1])],
                                     stage.at[k % n_stage], sem_ref.at[k % n_stage])

    for k in range(min(n_stage - 1, len(blocks))):
        copy(k).start()
    for k, (_, dst, stage, r) in enumerate(blocks):
        if k + n_stage - 1 < len(blocks):
            copy(k + n_stage - 1).start()
        copy(k).wait()
        dst[r:r + stage.shape[1], :] = stage[k % n_stage].astype(jnp.bfloat16)


def _first_grid_step():
    return (pl.program_id(0) == 0) & (pl.program_id(1) == 0)


def _mixer_kernel(x_ref, g_ref, w_in_hbm, conv_w_hbm, ln_g_ref, ln_b_ref,
                  ws_ref, bs_ref, w_out_hbm, o_ref,
                  w_in_ref, w_out_ref, stage_in_ref, stage_out_ref, sem_ref,
                  h_ref, proj_ref, carry_ref, mix_ref, bst_ref, conv_w_ref, conv_sem):
    tm, d = x_ref.shape
    n_col_chunks = d // CW
    n_row_blocks = tm // CHUNK
    heads_per_chunk = CW // HEAD_DIM

    @pl.when(_first_grid_step())
    def _():
        conv_copy = pltpu.make_async_copy(conv_w_hbm.at[0], conv_w_ref, conv_sem)
        conv_copy.start()
        _load_weights_bf16([(w_in_hbm, w_in_ref, stage_in_ref),
                            (w_out_hbm, w_out_ref, stage_out_ref)], sem_ref)
        conv_copy.wait()

    @pl.when(pl.program_id(1) == 0)
    def _():
        carry_ref[...] = jnp.zeros_like(carry_ref)

    h_ref[...] = _rms(x_ref[...], g_ref[...]).astype(jnp.bfloat16)
    bst_ref[...] = bs_ref[...].T

    tri = (lax.broadcasted_iota(jnp.int32, (CHUNK, CHUNK), 0)
           >= lax.broadcasted_iota(jnp.int32, (CHUNK, CHUNK), 1))

    def in_proj(j):
        for s in range(N_SECT):
            proj_ref[j % 2, :, s * CW:(s + 1) * CW] = jnp.dot(
                h_ref[...], w_in_ref[:, s * d + j * CW:s * d + (j + 1) * CW],
                preferred_element_type=jnp.float32)

    def out_proj(c0, c1):
        part = [jnp.dot(mix_ref[:, o + c0:o + c1], w_out_ref[o + c0:o + c1, :],
                        preferred_element_type=jnp.float32) for o in (0, d)]
        return part[0] + part[1]

    in_proj(0)
    for j in range(n_col_chunks):
        cols = slice(j * CW, (j + 1) * CW)
        if j + 1 < n_col_chunks:
            in_proj(j + 1)
        else:
            o_ref[...] = x_ref[...] + out_proj(0, j * CW)

        def sect(s, rows, j=j):
            return proj_ref[j % 2, rows, s * CW:(s + 1) * CW]

        w0 = conv_w_ref[0:1, cols]
        w1 = conv_w_ref[1:2, cols]
        w2 = conv_w_ref[2:3, cols]
        wc = [jnp.where(tri, ws_ref[j * heads_per_chunk + hh], 0.0).astype(jnp.bfloat16)
              for hh in range(heads_per_chunk)]

        for r in range(n_row_blocks):
            rows = slice(r * CHUNK, (r + 1) * CHUNK)
            g = sect(1, rows) * sect(2, rows)
            if r == 0:
                prev = carry_ref[:, cols]
            else:
                prows = slice(r * CHUNK - SUBLANES, r * CHUNK)
                prev = sect(1, prows) * sect(2, prows)
            if r == n_row_blocks - 1:
                carry_ref[:, cols] = g[CHUNK - SUBLANES:, :]
            ext = jnp.concatenate([prev, g], axis=0)
            g1 = ext[SUBLANES - 1:SUBLANES - 1 + CHUNK, :]
            g2 = ext[SUBLANES - 2:SUBLANES - 2 + CHUNK, :]
            conv = w0 * g2 + w1 * g1 + w2 * g
            za = sect(3, rows)
            a = sect(0, rows) * conv * (za * _sigmoid(za))
            mix_ref[rows, cols] = a.astype(jnp.bfloat16)

            u = _gelu_tanh(sect(4, rows))
            v = _gelu_tanh(sect(5, rows))
            zb = sect(6, rows)
            sps = []
            for hh in range(heads_per_chunk):
                hc = slice(hh * HEAD_DIM, (hh + 1) * HEAD_DIM)
                gcols = slice(j * CW + hh * HEAD_DIM, j * CW + (hh + 1) * HEAD_DIM)
                vh = v[:, hc]
                mu = jnp.mean(vh, axis=-1, keepdims=True)
                vc = vh - mu
                var = jnp.mean(vc * vc, axis=-1, keepdims=True)
                vn = vc * lax.rsqrt(var + EPS) * ln_g_ref[:, gcols] + ln_b_ref[:, gcols]
                sp = jnp.dot(wc[hh], vn.astype(jnp.bfloat16),
                             preferred_element_type=jnp.float32)
                head = j * heads_per_chunk + hh
                sps.append(sp + bst_ref[:, head:head + 1])
            sp = jnp.concatenate(sps, axis=-1)
            bo = u * sp * (zb * _sigmoid(zb))
            mix_ref[rows, d + j * CW:d + (j + 1) * CW] = bo.astype(jnp.bfloat16)

    o_ref[...] += out_proj((n_col_chunks - 1) * CW, d)


def _attn_kernel(x_ref, mem_ref, gx_ref, gm_ref, w_q_hbm, w_kv_hbm, w_xo_hbm, gf_ref,
                 o_ref,
                 w_q_ref, w_kv_ref, w_xo_ref, stage_d_ref, stage_kv_ref, sem_ref,
                 wqk_ref, vxo_ref, p_ref):
    tm, d = x_ref.shape
    m_len = mem_ref.shape[0]
    hd = d // X_HEADS
    bf16 = jnp.bfloat16

    @pl.when(_first_grid_step())
    def _():
        _load_weights_bf16([(w_q_hbm, w_q_ref, stage_d_ref),
                            (w_kv_hbm, w_kv_ref, stage_kv_ref),
                            (w_xo_hbm, w_xo_ref, stage_d_ref)], sem_ref)

    @pl.when(pl.program_id(1) == 0)
    def _():
        m = _rms(mem_ref[...], gm_ref[...]).astype(bf16)
        kv = jnp.dot(m, w_kv_ref[...], preferred_element_type=jnp.float32)
        for hh in range(X_HEADS):
            hc = slice(hh * hd, (hh + 1) * hd)
            mc = slice(hh * m_len, (hh + 1) * m_len)
            k_h = kv[:, hc].astype(bf16)
            v_h = kv[:, d + hh * hd:d + (hh + 1) * hd].astype(bf16)
            wqk = lax.dot_general(w_q_ref[:, hc], k_h, (((1,), (1,)), ((), ())),
                                  preferred_element_type=jnp.float32)
            wqk_ref[:, mc] = (wqk * (1.0 / math.sqrt(hd))).astype(bf16)
            vxo_ref[mc, :] = jnp.dot(v_h, w_xo_ref[hc, :],
                                     preferred_element_type=jnp.float32).astype(bf16)

    halves = [slice(i * ATTN_HALF, (i + 1) * ATTN_HALF) for i in range(tm // ATTN_HALF)]
    hs = [_rms(x_ref[r, :], gx_ref[...]).astype(bf16) for r in halves]
    scores = [jnp.dot(h, wqk_ref[...], preferred_element_type=jnp.float32) for h in hs]
    for hh in range(X_HEADS):
        mc = slice(hh * m_len, (hh + 1) * m_len)
        for s, r in zip(scores, halves):
            p = jnp.exp(s[:, mc] - jnp.max(s[:, mc], axis=-1, keepdims=True))
            l = jnp.sum(p, axis=-1, keepdims=True)
            p_ref[r, mc] = (p * (1.0 / l)).astype(bf16)
    for r in halves:
        y = x_ref[r, :] + jnp.dot(p_ref[r, :], vxo_ref[...],
                                  preferred_element_type=jnp.float32)
        o_ref[r, :] = _rms(y, gf_ref[...])


def _stage(cols, n_stage):
    rows = 1 << int(math.log2(STAGE_BYTES // (4 * cols)))
    return pltpu.VMEM((n_stage, rows, cols), jnp.float32)


def _resident(shape):
    return pl.BlockSpec(shape, lambda *_: (0,) * len(shape),
                        pipeline_mode=pl.Buffered(1))


def _resident_layer(shape):
    return pl.BlockSpec((None,) + tuple(shape[1:]), lambda *_: (0,) * len(shape),
                        pipeline_mode=pl.Buffered(1))


def kernel(x, mem, norm_mix_g, w_in, conv_w, gm_ln_g, gm_ln_b, gm_ws, gm_bs, w_out,
           norm_x_g, norm_mem_g, w_q, w_kv, w_xo, norm_final_g):
    b, s, d = x.shape
    m_len = mem.shape[1]
    assert w_in.shape[0] == 1, "the final norm is fused into the (single) layer's attention call"
    assert s % TM == 0 and TM % CHUNK == 0 and d % CW == 0
    assert s % TM_ATTN == 0 and TM_ATTN % ATTN_HALF == 0
    bf16 = jnp.bfloat16
    f32 = jnp.float32
    params = lambda limit: pltpu.CompilerParams(
        dimension_semantics=("arbitrary", "arbitrary"), vmem_limit_bytes=limit)
    row = lambda a: a.reshape(1, -1)
    hbm = pl.BlockSpec(memory_space=pl.ANY)
    tile = pl.BlockSpec((None, TM, d), lambda i, t: (i, t, 0))
    sems = lambda n: pltpu.SemaphoreType.DMA((n,))

    x = pl.pallas_call(
        _mixer_kernel,
        grid=(b, s // TM),
        in_specs=[tile,
                  _resident((1, d)),
                  hbm,
                  hbm,
                  _resident((1, d)), _resident((1, d)),
                  _resident(gm_ws.shape[1:]),
                  _resident_layer(gm_bs.shape),
                  hbm],
        out_specs=tile,
        out_shape=jax.ShapeDtypeStruct((b, s, d), f32),
        scratch_shapes=[pltpu.VMEM((d, N_SECT * d), bf16),
                        pltpu.VMEM((2 * d, d), bf16),
                        _stage(N_SECT * d, MIX_STAGES), _stage(d, MIX_STAGES),
                        sems(MIX_STAGES),
                        pltpu.VMEM((TM, d), bf16),
                        pltpu.VMEM((2, TM, N_SECT * CW), f32),
                        pltpu.VMEM((SUBLANES, d), f32),
                        pltpu.VMEM((TM, 2 * d), bf16),
                        pltpu.VMEM((CHUNK, gm_bs.shape[1]), f32),
                        pltpu.VMEM(conv_w.shape[1:], f32),
                        pltpu.SemaphoreType.DMA(())],
        compiler_params=params(MIX_VMEM_LIMIT),
        name="mixer",
    )(x, row(norm_mix_g[0]), w_in[0], conv_w, row(gm_ln_g[0]), row(gm_ln_b[0]),
      gm_ws[0], gm_bs, w_out[0])

    attn_tile = pl.BlockSpec((None, TM_ATTN, d), lambda i, t: (i, t, 0))
    return pl.pallas_call(
        _attn_kernel,
        grid=(b, s // TM_ATTN),
        in_specs=[attn_tile,
                  pl.BlockSpec((None, m_len, d), lambda i, t: (i, 0, 0)),
                  _resident((1, d)), _resident((1, d)),
                  hbm, hbm, hbm,
                  _resident((1, d))],
        out_specs=attn_tile,
        out_shape=jax.ShapeDtypeStruct((b, s, d), f32),
        scratch_shapes=[pltpu.VMEM((d, d), bf16),
                        pltpu.VMEM((d, 2 * d), bf16),
                        pltpu.VMEM((d, d), bf16),
                        _stage(d, ATTN_STAGES), _stage(2 * d, ATTN_STAGES),
                        sems(ATTN_STAGES),
                        pltpu.VMEM((d, X_HEADS * m_len), bf16),
                        pltpu.VMEM((X_HEADS * m_len, d), bf16),
                        pltpu.VMEM((TM_ATTN, X_HEADS * m_len), bf16)],
        compiler_params=params(ATTN_VMEM_LIMIT),
        name="xattn",
    )(x, mem, row(norm_x_g[0]), row(norm_mem_g[0]), w_q[0], w_kv[0], w_xo[0],
      row(norm_final_g))
```

```python
import math

import jax
import jax.numpy as jnp
from jax import lax
from jax.experimental import pallas as pl
from jax.experimental.pallas import tpu as pltpu

EPS = 1e-6
CONV_K = 3
CHUNK = 128
HEAD_DIM = 128
N_SECT = 7
X_HEADS = 4
SUBLANES = 8

TM = 1024
TM_ATTN = 1024
ATTN_HALF = 512
CW = 256
MIX_STAGING = (3, 1 << 20)
ATTN_STAGING = (3, 1 << 21)
MIX_VMEM_LIMIT = 60 * 1024 * 1024
ATTN_VMEM_LIMIT = 56 * 1024 * 1024


def _rms(x, g):
    ms = jnp.mean(x * x, axis=-1, keepdims=True)
    return x * lax.rsqrt(ms + EPS) * g


def _sigmoid(z):
    return 0.5 * (1.0 + jnp.tanh(0.5 * z))


def _gelu_tanh(x):
    c = math.sqrt(2.0 / math.pi)
    return 0.5 * x * (1.0 + jnp.tanh(c * (x + 0.044715 * (x * x * x))))


def _load_weights_bf16(jobs, sem_ref):
    n_stage = sem_ref.shape[0]
    blocks = [(src, dst, stage, r)
              for src, dst, stage in jobs
              for r in range(0, src.shape[0], stage.shape[1])]

    def copy(k):
        src, _, stage, r = blocks[k]
        return pltpu.make_async_copy(src.at[pl.ds(r, stage.shape[1])],
                                     stage.at[k % n_stage], sem_ref.at[k % n_stage])

    for k in range(min(n_stage - 1, len(blocks))):
        copy(k).start()
    for k, (_, dst, stage, r) in enumerate(blocks):
        if k + n_stage - 1 < len(blocks):
            copy(k + n_stage - 1).start()
        copy(k).wait()
        dst[r:r + stage.shape[1], :] = stage[k % n_stage].astype(jnp.bfloat16)


def _first_grid_step():
    return (pl.program_id(0) == 0) & (pl.program_id(1) == 0)


def _mixer_kernel(x_ref, g_ref, w_in_hbm, conv_w_hbm, ln_g_ref, ln_b_ref,
                  ws_ref, bs_ref, w_out_hbm, o_ref,
                  w_in_ref, w_out_ref, stage_in_ref, stage_out_ref, sem_ref,
                  h_ref, proj_ref, carry_ref, mix_ref, bst_ref, conv_w_ref, conv_sem):
    tm, d = x_ref.shape
    n_col_chunks = d // CW
    n_row_blocks = tm // CHUNK
    heads_per_chunk = CW // HEAD_DIM

    @pl.when(_first_grid_step())
    def _():
        conv_copy = pltpu.make_async_copy(conv_w_hbm.at[0], conv_w_ref, conv_sem)
        conv_copy.start()
        _load_weights_bf16([(w_in_hbm, w_in_ref, stage_in_ref),
                            (w_out_hbm, w_out_ref, stage_out_ref)], sem_ref)
        conv_copy.wait()

    @pl.when(pl.program_id(1) == 0)
    def _():
        carry_ref[...] = jnp.zeros_like(carry_ref)

    h_ref[...] = _rms(x_ref[...], g_ref[...]).astype(jnp.bfloat16)
    bst_ref[...] = bs_ref[...].T

    tri = (lax.broadcasted_iota(jnp.int32, (CHUNK, CHUNK), 0)
           >= lax.broadcasted_iota(jnp.int32, (CHUNK, CHUNK), 1))

    def in_proj(j):
        for s in range(N_SECT):
            proj_ref[j % 2, :, s * CW:(s + 1) * CW] = jnp.dot(
                h_ref[...], w_in_ref[:, s * d + j * CW:s * d + (j + 1) * CW],
                preferred_element_type=jnp.float32)

    def out_proj(c0, c1):
        part = [jnp.dot(mix_ref[:, o + c0:o + c1], w_out_ref[o + c0:o + c1, :],
                        preferred_element_type=jnp.float32) for o in (0, d)]
        return part[0] + part[1]

    in_proj(0)
    for j in range(n_col_chunks):
        cols = slice(j * CW, (j + 1) * CW)
        if j + 1 < n_col_chunks:
            in_proj(j + 1)
        else:
            o_ref[...] = x_ref[...] + out_proj(0, j * CW)

        def sect(s, rows, j=j):
            return proj_ref[j % 2, rows, s * CW:(s + 1) * CW]

        w0 = conv_w_ref[0:1, cols]
        w1 = conv_w_ref[1:2, cols]
        w2 = conv_w_ref[2:3, cols]
        wc = [jnp.where(tri, ws_ref[j * heads_per_chunk + hh], 0.0).astype(jnp.bfloat16)
              for hh in range(heads_per_chunk)]

        for r in range(n_row_blocks):
            rows = slice(r * CHUNK, (r + 1) * CHUNK)
            g = sect(1, rows) * sect(2, rows)
            if r == 0:
                prev = carry_ref[:, cols]
            else:
                prows = slice(r * CHUNK - SUBLANES, r * CHUNK)
                prev = sect(1, prows) * sect(2, prows)
            if r == n_row_blocks - 1:
                carry_ref[:, cols] = g[CHUNK - SUBLANES:, :]
            ext = jnp.concatenate([prev, g], axis=0)
            g1 = ext[SUBLANES - 1:SUBLANES - 1 + CHUNK, :]
            g2 = ext[SUBLANES - 2:SUBLANES - 2 + CHUNK, :]
            conv = w0 * g2 + w1 * g1 + w2 * g
            za = sect(3, rows)
            a = sect(0, rows) * conv * (za * _sigmoid(za))
            mix_ref[rows, cols] = a.astype(jnp.bfloat16)

            u = _gelu_tanh(sect(4, rows))
            v = _gelu_tanh(sect(5, rows))
            zb = sect(6, rows)
            sps = []
            for hh in range(heads_per_chunk):
                hc = slice(hh * HEAD_DIM, (hh + 1) * HEAD_DIM)
                gcols = slice(j * CW + hh * HEAD_DIM, j * CW + (hh + 1) * HEAD_DIM)
                vh = v[:, hc]
                mu = jnp.mean(vh, axis=-1, keepdims=True)
                vc = vh - mu
                var = jnp.mean(vc * vc, axis=-1, keepdims=True)
                vn = vc * lax.rsqrt(var + EPS) * ln_g_ref[:, gcols] + ln_b_ref[:, gcols]
                sp = jnp.dot(wc[hh], vn.astype(jnp.bfloat16),
                             preferred_element_type=jnp.float32)
                head = j * heads_per_chunk + hh
                sps.append(sp + bst_ref[:, head:head + 1])
            sp = jnp.concatenate(sps, axis=-1)
            bo = u * sp * (zb * _sigmoid(zb))
            mix_ref[rows, d + j * CW:d + (j + 1) * CW] = bo.astype(jnp.bfloat16)

    o_ref[...] += out_proj((n_col_chunks - 1) * CW, d)


def _attn_kernel(x_ref, mem_ref, gx_ref, gm_ref, w_q_hbm, w_kv_hbm, w_xo_hbm, gf_ref,
                 o_ref,
                 w_q_ref, w_kv_ref, w_xo_ref, stage_d_ref, stage_kv_ref, sem_ref,
                 wqk_ref, vxo_ref, p_ref):
    tm, d = x_ref.shape
    m_len = mem_ref.shape[0]
    hd = d // X_HEADS
    bf16 = jnp.bfloat16

    @pl.when(_first_grid_step())
    def _():
        _load_weights_bf16([(w_q_hbm, w_q_ref, stage_d_ref),
                            (w_kv_hbm, w_kv_ref, stage_kv_ref),
                            (w_xo_hbm, w_xo_ref, stage_d_ref)], sem_ref)

    @pl.when(pl.program_id(1) == 0)
    def _():
        m = _rms(mem_ref[...], gm_ref[...]).astype(bf16)
        kv = jnp.dot(m, w_kv_ref[...], preferred_element_type=jnp.float32)
        for hh in range(X_HEADS):
            hc = slice(hh * hd, (hh + 1) * hd)
            mc = slice(hh * m_len, (hh + 1) * m_len)
            k_h = kv[:, hc].astype(bf16)
            v_h = kv[:, d + hh * hd:d + (hh + 1) * hd].astype(bf16)
            wqk = lax.dot_general(w_q_ref[:, hc], k_h, (((1,), (1,)), ((), ())),
                                  preferred_element_type=jnp.float32)
            wqk_ref[:, mc] = (wqk * (1.0 / math.sqrt(hd))).astype(bf16)
            vxo_ref[mc, :] = jnp.dot(v_h, w_xo_ref[hc, :],
                                     preferred_element_type=jnp.float32).astype(bf16)

    halves = [slice(i * ATTN_HALF, (i + 1) * ATTN_HALF) for i in range(tm // ATTN_HALF)]
    hs = [_rms(x_ref[r, :], gx_ref[...]).astype(bf16) for r in halves]
    scores = [jnp.dot(h, wqk_ref[...], preferred_element_type=jnp.float32) for h in hs]
    for hh in range(X_HEADS):
        mc = slice(hh * m_len, (hh + 1) * m_len)
        for s, r in zip(scores, halves):
            p = jnp.exp(s[:, mc] - jnp.max(s[:, mc], axis=-1, keepdims=True))
            l = jnp.sum(p, axis=-1, keepdims=True)
            p_ref[r, mc] = (p * (1.0 / l)).astype(bf16)
    for r in halves:
        y = x_ref[r, :] + jnp.dot(p_ref[r, :], vxo_ref[...],
                                  preferred_element_type=jnp.float32)
        o_ref[r, :] = _rms(y, gf_ref[...])


def _stage(cols, staging):
    n_stage, slot_bytes = staging
    rows = 1 << int(math.log2(slot_bytes // (4 * cols)))
    return pltpu.VMEM((n_stage, rows, cols), jnp.float32)


def _resident(shape):
    return pl.BlockSpec(shape, lambda *_: (0,) * len(shape),
                        pipeline_mode=pl.Buffered(1))


def _resident_layer(shape):
    return pl.BlockSpec((None,) + tuple(shape[1:]), lambda *_: (0,) * len(shape),
                        pipeline_mode=pl.Buffered(1))


def kernel(x, mem, norm_mix_g, w_in, conv_w, gm_ln_g, gm_ln_b, gm_ws, gm_bs, w_out,
           norm_x_g, norm_mem_g, w_q, w_kv, w_xo, norm_final_g):
    b, s, d = x.shape
    m_len = mem.shape[1]
    assert w_in.shape[0] == 1, "the final norm is fused into the (single) layer's attention call"
    assert s % TM == 0 and TM % CHUNK == 0 and d % CW == 0
    assert s % TM_ATTN == 0 and TM_ATTN % ATTN_HALF == 0
    bf16 = jnp.bfloat16
    f32 = jnp.float32
    params = lambda limit: pltpu.CompilerParams(
        dimension_semantics=("arbitrary", "arbitrary"), vmem_limit_bytes=limit)
    row = lambda a: a.reshape(1, -1)
    hbm = pl.BlockSpec(memory_space=pl.ANY)
    tile = pl.BlockSpec((None, TM, d), lambda i, t: (i, t, 0))
    sems = lambda n: pltpu.SemaphoreType.DMA((n,))

    x = pl.pallas_call(
        _mixer_kernel,
        grid=(b, s // TM),
        in_specs=[tile,
                  _resident((1, d)),
                  hbm,
                  hbm,
                  _resident((1, d)), _resident((1, d)),
                  _resident(gm_ws.shape[1:]),
                  _resident_layer(gm_bs.shape),
                  hbm],
        out_specs=tile,
        out_shape=jax.ShapeDtypeStruct((b, s, d), f32),
        scratch_shapes=[pltpu.VMEM((d, N_SECT * d), bf16),
                        pltpu.VMEM((2 * d, d), bf16),
                        _stage(N_SECT * d, MIX_STAGING), _stage(d, MIX_STAGING),
                        sems(MIX_STAGING[0]),
                        pltpu.VMEM((TM, d), bf16),
                        pltpu.VMEM((2, TM, N_SECT * CW), f32),
                        pltpu.VMEM((SUBLANES, d), f32),
                        pltpu.VMEM((TM, 2 * d), bf16),
                        pltpu.VMEM((CHUNK, gm_bs.shape[1]), f32),
                        pltpu.VMEM(conv_w.shape[1:], f32),
                        pltpu.SemaphoreType.DMA(())],
        compiler_params=params(MIX_VMEM_LIMIT),
        name="mixer",
    )(x, row(norm_mix_g[0]), w_in[0], conv_w, row(gm_ln_g[0]), row(gm_ln_b[0]),
      gm_ws[0], gm_bs, w_out[0])

    attn_tile = pl.BlockSpec((None, TM_ATTN, d), lambda i, t: (i, t, 0))
    return pl.pallas_call(
        _attn_kernel,
        grid=(b, s // TM_ATTN),
        in_specs=[attn_tile,
                  pl.BlockSpec((None, m_len, d), lambda i, t: (i, 0, 0)),
                  _resident((1, d)), _resident((1, d)),
                  hbm, hbm, hbm,
                  _resident((1, d))],
        out_specs=attn_tile,
        out_shape=jax.ShapeDtypeStruct((b, s, d), f32),
        scratch_shapes=[pltpu.VMEM((d, d), bf16),
                        pltpu.VMEM((d, 2 * d), bf16),
                        pltpu.VMEM((d, d), bf16),
                        _stage(d, ATTN_STAGING), _stage(2 * d, ATTN_STAGING),
                        sems(ATTN_STAGING[0]),
                        pltpu.VMEM((d, X_HEADS * m_len), bf16),
                        pltpu.VMEM((X_HEADS * m_len, d), bf16),
                        pltpu.VMEM((TM_ATTN, X_HEADS * m_len), bf16)],
        compiler_params=params(ATTN_VMEM_LIMIT),
        name="xattn",
    )(x, mem, row(norm_x_g[0]), row(norm_mem_g[0]), w_q[0], w_kv[0], w_xo[0],
      row(norm_final_g))
```

```python
import math

import jax
import jax.numpy as jnp
from jax import lax
from jax.experimental import pallas as pl
from jax.experimental.pallas import tpu as pltpu

EPS = 1e-6
CONV_K = 3
CHUNK = 128
HEAD_DIM = 128
N_SECT = 7
X_HEADS = 4
SUBLANES = 8

TM = 1024
TM_ATTN = 1024
ATTN_HALF = 512
CW = 256
MIX_STAGES = 2
ATTN_STAGES = 3
STAGE_BYTES = 1 << 21
MIX_VMEM_LIMIT = 60 * 1024 * 1024
ATTN_VMEM_LIMIT = 56 * 1024 * 1024
PARAM_ROWS = 4


def _rms(x, g):
    ms = jnp.mean(x * x, axis=-1, keepdims=True)
    return x * lax.rsqrt(ms + EPS) * g


def _sigmoid(z):
    return 0.5 * (1.0 + jnp.tanh(0.5 * z))


def _gelu_tanh(x):
    c = math.sqrt(2.0 / math.pi)
    return 0.5 * x * (1.0 + jnp.tanh(c * (x + 0.044715 * (x * x * x))))


def _load_weights_bf16(jobs, sem_ref):
    n_stage = sem_ref.shape[0]
    blocks = [(src, dst, stage, r)
              for src, dst, stage in jobs
              for r in range(0, src.shape[0], stage.shape[1])]

    def copy(k):
        src, _, stage, r = blocks[k]
        return pltpu.make_async_copy(src.at[pl.ds(r, stage.shape[1])],
                                     stage.at[k % n_stage], sem_ref.at[k % n_stage])

    for k in range(min(n_stage - 1, len(blocks))):
        copy(k).start()
    for k, (_, dst, stage, r) in enumerate(blocks):
        if k + n_stage - 1 < len(blocks):
            copy(k + n_stage - 1).start()
        copy(k).wait()
        dst[r:r + stage.shape[1], :] = stage[k % n_stage].astype(jnp.bfloat16)


def _first_grid_step():
    return (pl.program_id(0) == 0) & (pl.program_id(1) == 0)


def _mixer_kernel(x_ref, g_ref, w_in_hbm, conv_w_hbm, ln_g_ref, ln_b_ref,
                  ws_ref, bs_ref, w_out_hbm, o_ref,
                  w_in_ref, w_out_ref, stage_in_ref, stage_out_ref, sem_ref,
                  h_ref, proj_ref, carry_ref, mix_ref, bst_ref, conv_w_ref, conv_sem):
    tm, d = x_ref.shape
    n_col_chunks = d // CW
    n_row_blocks = tm // CHUNK
    heads_per_chunk = CW // HEAD_DIM

    @pl.when(_first_grid_step())
    def _():
        conv_copy = pltpu.make_async_copy(conv_w_hbm, conv_w_ref.at[pl.ds(0, CONV_K)], conv_sem)
        conv_copy.start()
        _load_weights_bf16([(w_in_hbm, w_in_ref, stage_in_ref),
                            (w_out_hbm, w_out_ref, stage_out_ref)], sem_ref)
        conv_copy.wait()

    @pl.when(pl.program_id(1) == 0)
    def _():
        carry_ref[...] = jnp.zeros_like(carry_ref)

    h_ref[...] = _rms(x_ref[...], g_ref[...]).astype(jnp.bfloat16)
    bst_ref[...] = bs_ref[...].T

    tri = (lax.broadcasted_iota(jnp.int32, (CHUNK, CHUNK), 0)
           >= lax.broadcasted_iota(jnp.int32, (CHUNK, CHUNK), 1))

    def in_proj(j):
        for s in range(N_SECT):
            proj_ref[j % 2, :, s * CW:(s + 1) * CW] = jnp.dot(
                h_ref[...], w_in_ref[:, s * d + j * CW:s * d + (j + 1) * CW],
                preferred_element_type=jnp.float32)

    def out_proj(c0, c1):
        part = [jnp.dot(mix_ref[:, o + c0:o + c1], w_out_ref[o + c0:o + c1, :],
                        preferred_element_type=jnp.float32) for o in (0, d)]
        return part[0] + part[1]

    in_proj(0)
    for j in range(n_col_chunks):
        cols = slice(j * CW, (j + 1) * CW)
        if j + 1 < n_col_chunks:
            in_proj(j + 1)
        else:
            o_ref[...] = x_ref[...] + out_proj(0, j * CW)

        def sect(s, rows, j=j):
            return proj_ref[j % 2, rows, s * CW:(s + 1) * CW]

        w0 = conv_w_ref[0, :, cols]
        w1 = conv_w_ref[1, :, cols]
        w2 = conv_w_ref[2, :, cols]
        wc = [jnp.where(tri, ws_ref[j * heads_per_chunk + hh], 0.0).astype(jnp.bfloat16)
              for hh in range(heads_per_chunk)]

        for r in range(n_row_blocks):
            rows = slice(r * CHUNK, (r + 1) * CHUNK)
            g = sect(1, rows) * sect(2, rows)
            if r == 0:
                prev = carry_ref[:, cols]
            else:
                prows = slice(r * CHUNK - SUBLANES, r * CHUNK)
                prev = sect(1, prows) * sect(2, prows)
            if r == n_row_blocks - 1:
                carry_ref[:, cols] = g[CHUNK - SUBLANES:, :]
            ext = jnp.concatenate([prev, g], axis=0)
            g1 = ext[SUBLANES - 1:SUBLANES - 1 + CHUNK, :]
            g2 = ext[SUBLANES - 2:SUBLANES - 2 + CHUNK, :]
            conv = w0 * g2 + w1 * g1 + w2 * g
            za = sect(3, rows)
            a = sect(0, rows) * conv * (za * _sigmoid(za))
            mix_ref[rows, cols] = a.astype(jnp.bfloat16)

            u = _gelu_tanh(sect(4, rows))
            v = _gelu_tanh(sect(5, rows))
            zb = sect(6, rows)
            sps = []
            for hh in range(heads_per_chunk):
                hc = slice(hh * HEAD_DIM, (hh + 1) * HEAD_DIM)
                gcols = slice(j * CW + hh * HEAD_DIM, j * CW + (hh + 1) * HEAD_DIM)
                vh = v[:, hc]
                mu = jnp.mean(vh, axis=-1, keepdims=True)
                vc = vh - mu
                var = jnp.mean(vc * vc, axis=-1, keepdims=True)
                vn = vc * lax.rsqrt(var + EPS) * ln_g_ref[:, gcols] + ln_b_ref[:, gcols]
                sp = jnp.dot(wc[hh], vn.astype(jnp.bfloat16),
                             preferred_element_type=jnp.float32)
                head = j * heads_per_chunk + hh
                sps.append(sp + bst_ref[:, head:head + 1])
            sp = jnp.concatenate(sps, axis=-1)
            bo = u * sp * (zb * _sigmoid(zb))
            mix_ref[rows, d + j * CW:d + (j + 1) * CW] = bo.astype(jnp.bfloat16)

    o_ref[...] += out_proj((n_col_chunks - 1) * CW, d)


def _attn_kernel(x_ref, mem_ref, gx_ref, gm_ref, w_q_hbm, w_kv_hbm, w_xo_hbm, gf_ref,
                 o_ref,
                 w_q_ref, w_kv_ref, w_xo_ref, stage_d_ref, stage_kv_ref, sem_ref,
                 wqk_ref, vxo_ref, p_ref):
    tm, d = x_ref.shape
    m_len = mem_ref.shape[0]
    hd = d // X_HEADS
    bf16 = jnp.bfloat16

    @pl.when(_first_grid_step())
    def _():
        _load_weights_bf16([(w_q_hbm, w_q_ref, stage_d_ref),
                            (w_kv_hbm, w_kv_ref, stage_kv_ref),
                            (w_xo_hbm, w_xo_ref, stage_d_ref)], sem_ref)

    @pl.when(pl.program_id(1) == 0)
    def _():
        m = _rms(mem_ref[...], gm_ref[...]).astype(bf16)
        kv = jnp.dot(m, w_kv_ref[...], preferred_element_type=jnp.float32)
        for hh in range(X_HEADS):
            hc = slice(hh * hd, (hh + 1) * hd)
            mc = slice(hh * m_len, (hh + 1) * m_len)
            k_h = kv[:, hc].astype(bf16)
            v_h = kv[:, d + hh * hd:d + (hh + 1) * hd].astype(bf16)
            wqk = lax.dot_general(w_q_ref[:, hc], k_h, (((1,), (1,)), ((), ())),
                                  preferred_element_type=jnp.float32)
            wqk_ref[:, mc] = (wqk * (1.0 / math.sqrt(hd))).astype(bf16)
            vxo_ref[mc, :] = jnp.dot(v_h, w_xo_ref[hc, :],
                                     preferred_element_type=jnp.float32).astype(bf16)

    halves = [slice(i * ATTN_HALF, (i + 1) * ATTN_HALF) for i in range(tm // ATTN_HALF)]
    hs = [_rms(x_ref[r, :], gx_ref[...]).astype(bf16) for r in halves]
    scores = [jnp.dot(h, wqk_ref[...], preferred_element_type=jnp.float32) for h in hs]
    for hh in range(X_HEADS):
        mc = slice(hh * m_len, (hh + 1) * m_len)
        for s, r in zip(scores, halves):
            p = jnp.exp(s[:, mc] - jnp.max(s[:, mc], axis=-1, keepdims=True))
            l = jnp.sum(p, axis=-1, keepdims=True)
            p_ref[r, mc] = (p * (1.0 / l)).astype(bf16)
    for r in halves:
        y = x_ref[r, :] + jnp.dot(p_ref[r, :], vxo_ref[...],
                                  preferred_element_type=jnp.float32)
        o_ref[r, :] = _rms(y, gf_ref[...])


def _stage(cols, n_stage):
    rows = 1 << int(math.log2(STAGE_BYTES // (4 * cols)))
    return pltpu.VMEM((n_stage, rows, cols), jnp.float32)


def _resident(shape):
    return pl.BlockSpec(shape, lambda *_: (0,) * len(shape),
                        pipeline_mode=pl.Buffered(1))


def _resident_layer(shape):
    return pl.BlockSpec((None,) + tuple(shape[1:]), lambda *_: (0,) * len(shape),
                        pipeline_mode=pl.Buffered(1))


def kernel(x, mem, norm_mix_g, w_in, conv_w, gm_ln_g, gm_ln_b, gm_ws, gm_bs, w_out,
           norm_x_g, norm_mem_g, w_q, w_kv, w_xo, norm_final_g):
    b, s, d = x.shape
    m_len = mem.shape[1]
    assert w_in.shape[0] == 1, "the final norm is fused into the (single) layer's attention call"
    assert s % TM == 0 and TM % CHUNK == 0 and d % CW == 0
    assert s % TM_ATTN == 0 and TM_ATTN % ATTN_HALF == 0
    bf16 = jnp.bfloat16
    f32 = jnp.float32
    params = lambda limit: pltpu.CompilerParams(
        dimension_semantics=("arbitrary", "arbitrary"), vmem_limit_bytes=limit)
    row = lambda a: a.reshape(1, -1)
    hbm = pl.BlockSpec(memory_space=pl.ANY)
    tile = pl.BlockSpec((None, TM, d), lambda i, t: (i, t, 0))
    sems = lambda n: pltpu.SemaphoreType.DMA((n,))

    x = pl.pallas_call(
        _mixer_kernel,
        grid=(b, s // TM),
        in_specs=[tile,
                  _resident((1, d)),
                  hbm,
                  hbm,
                  _resident((1, d)), _resident((1, d)),
                  _resident(gm_ws.shape[1:]),
                  _resident_layer(gm_bs.shape),
                  hbm],
        out_specs=tile,
        out_shape=jax.ShapeDtypeStruct((b, s, d), f32),
        scratch_shapes=[pltpu.VMEM((d, N_SECT * d), bf16),
                        pltpu.VMEM((2 * d, d), bf16),
                        _stage(N_SECT * d, MIX_STAGES), _stage(d, MIX_STAGES),
                        sems(MIX_STAGES),
                        pltpu.VMEM((TM, d), bf16),
                        pltpu.VMEM((2, TM, N_SECT * CW), f32),
                        pltpu.VMEM((SUBLANES, d), f32),
                        pltpu.VMEM((TM, 2 * d), bf16),
                        pltpu.VMEM((CHUNK, gm_bs.shape[1]), f32),
                        pltpu.VMEM((PARAM_ROWS, 1, d), f32),
                        pltpu.SemaphoreType.DMA(())],
        compiler_params=params(MIX_VMEM_LIMIT),
        name="mixer",
    )(x, row(norm_mix_g[0]), w_in[0], conv_w.reshape(CONV_K, 1, d), row(gm_ln_g[0]), row(gm_ln_b[0]),
      gm_ws[0], gm_bs, w_out[0])

    attn_tile = pl.BlockSpec((None, TM_ATTN, d), lambda i, t: (i, t, 0))
    return pl.pallas_call(
        _attn_kernel,
        grid=(b, s // TM_ATTN),
        in_specs=[attn_tile,
                  pl.BlockSpec((None, m_len, d), lambda i, t: (i, 0, 0)),
                  _resident((1, d)), _resident((1, d)),
                  hbm, hbm, hbm,
                  _resident((1, d))],
        out_specs=attn_tile,
        out_shape=jax.ShapeDtypeStruct((b, s, d), f32),
        scratch_shapes=[pltpu.VMEM((d, d), bf16),
                        pltpu.VMEM((d, 2 * d), bf16),
                        pltpu.VMEM((d, d), bf16),
                        _stage(d, ATTN_STAGES), _stage(2 * d, ATTN_STAGES),
                        sems(ATTN_STAGES),
                        pltpu.VMEM((d, X_HEADS * m_len), bf16),
                        pltpu.VMEM((X_HEADS * m_len, d), bf16),
                        pltpu.VMEM((TM_ATTN, X_HEADS * m_len), bf16)],
        compiler_params=params(ATTN_VMEM_LIMIT),
        name="xattn",
    )(x, mem, row(norm_x_g[0]), row(norm_mem_g[0]), w_q[0], w_kv[0], w_xo[0],
      row(norm_final_g))
```

```python
import math

import jax
import jax.numpy as jnp
from jax import lax
from jax.experimental import pallas as pl
from jax.experimental.pallas import tpu as pltpu

EPS = 1e-6
CONV_K = 3
CHUNK = 128
HEAD_DIM = 128
N_SECT = 7
X_HEADS = 4
SUBLANES = 8

TM = 1024
TM_ATTN = 1024
ATTN_HALF = 512
CW = 256
MIX_STAGES = 3
ATTN_STAGES = 3
STAGE_BYTES = 1 << 21
MIX_VMEM_LIMIT = 63 * 1024 * 1024
ATTN_VMEM_LIMIT = 56 * 1024 * 1024
PARAM_ROWS = 4


def _rms(x, g):
    ms = jnp.mean(x * x, axis=-1, keepdims=True)
    return x * lax.rsqrt(ms + EPS) * g


def _sigmoid(z):
    return 0.5 * (1.0 + jnp.tanh(0.5 * z))


def _gelu_tanh(x):
    c = math.sqrt(2.0 / math.pi)
    return 0.5 * x * (1.0 + jnp.tanh(c * (x + 0.044715 * (x * x * x))))


def _load_weights_bf16(jobs, sem_ref):
    n_stage = sem_ref.shape[0]
    blocks = [(src, dst, stage, r)
              for src, dst, stage in jobs
              for r in range(0, src.shape[0], stage.shape[1])]

    def copy(k):
        src, _, stage, r = blocks[k]
        return pltpu.make_async_copy(src.at[pl.ds(r, stage.shape[1])],
                                     stage.at[k % n_stage], sem_ref.at[k % n_stage])

    for k in range(min(n_stage - 1, len(blocks))):
        copy(k).start()
    for k, (_, dst, stage, r) in enumerate(blocks):
        if k + n_stage - 1 < len(blocks):
            copy(k + n_stage - 1).start()
        copy(k).wait()
        dst[r:r + stage.shape[1], :] = stage[k % n_stage].astype(jnp.bfloat16)


def _first_grid_step():
    return (pl.program_id(0) == 0) & (pl.program_id(1) == 0)


def _mixer_kernel(x_ref, g_ref, w_in_hbm, conv_w_hbm, ln_g_ref, ln_b_ref,
                  ws_ref, bs_ref, w_out_hbm, o_ref,
                  w_in_ref, w_out_ref, stage_in_ref, stage_out_ref, sem_ref,
                  h_ref, proj_ref, carry_ref, mix_ref, bst_ref, conv_w_ref, conv_sem):
    tm, d = x_ref.shape
    n_col_chunks = d // CW
    n_row_blocks = tm // CHUNK
    heads_per_chunk = CW // HEAD_DIM

    @pl.when(_first_grid_step())
    def _():
        conv_copy = pltpu.make_async_copy(conv_w_hbm, conv_w_ref.at[pl.ds(0, CONV_K)], conv_sem)
        conv_copy.start()
        _load_weights_bf16([(w_in_hbm, w_in_ref, stage_in_ref),
                            (w_out_hbm, w_out_ref, stage_out_ref)], sem_ref)
        conv_copy.wait()

    @pl.when(pl.program_id(1) == 0)
    def _():
        carry_ref[...] = jnp.zeros_like(carry_ref)

    h_ref[...] = _rms(x_ref[...], g_ref[...]).astype(jnp.bfloat16)
    bst_ref[...] = bs_ref[...].T

    tri = (lax.broadcasted_iota(jnp.int32, (CHUNK, CHUNK), 0)
           >= lax.broadcasted_iota(jnp.int32, (CHUNK, CHUNK), 1))

    def in_proj(j):
        for s in range(N_SECT):
            proj_ref[j % 2, :, s * CW:(s + 1) * CW] = jnp.dot(
                h_ref[...], w_in_ref[:, s * d + j * CW:s * d + (j + 1) * CW],
                preferred_element_type=jnp.float32)

    def out_proj(c0, c1):
        part = [jnp.dot(mix_ref[:, o + c0:o + c1], w_out_ref[o + c0:o + c1, :],
                        preferred_element_type=jnp.float32) for o in (0, d)]
        return part[0] + part[1]

    in_proj(0)
    for j in range(n_col_chunks):
        cols = slice(j * CW, (j + 1) * CW)
        if j + 1 < n_col_chunks:
            in_proj(j + 1)
        else:
            o_ref[...] = x_ref[...] + out_proj(0, j * CW)

        def sect(s, rows, j=j):
            return proj_ref[j % 2, rows, s * CW:(s + 1) * CW]

        w0 = conv_w_ref[0, :, cols]
        w1 = conv_w_ref[1, :, cols]
        w2 = conv_w_ref[2, :, cols]
        wc = [jnp.where(tri, ws_ref[j * heads_per_chunk + hh], 0.0).astype(jnp.bfloat16)
              for hh in range(heads_per_chunk)]

        for r in range(n_row_blocks):
            rows = slice(r * CHUNK, (r + 1) * CHUNK)
            g = sect(1, rows) * sect(2, rows)
            if r == 0:
                prev = carry_ref[:, cols]
            else:
                prows = slice(r * CHUNK - SUBLANES, r * CHUNK)
                prev = sect(1, prows) * sect(2, prows)
            if r == n_row_blocks - 1:
                carry_ref[:, cols] = g[CHUNK - SUBLANES:, :]
            ext = jnp.concatenate([prev, g], axis=0)
            g1 = ext[SUBLANES - 1:SUBLANES - 1 + CHUNK, :]
            g2 = ext[SUBLANES - 2:SUBLANES - 2 + CHUNK, :]
            conv = w0 * g2 + w1 * g1 + w2 * g
            za = sect(3, rows)
            a = sect(0, rows) * conv * (za * _sigmoid(za))
            mix_ref[rows, cols] = a.astype(jnp.bfloat16)

            u = _gelu_tanh(sect(4, rows))
            v = _gelu_tanh(sect(5, rows))
            zb = sect(6, rows)
            sps = []
            for hh in range(heads_per_chunk):
                hc = slice(hh * HEAD_DIM, (hh + 1) * HEAD_DIM)
                gcols = slice(j * CW + hh * HEAD_DIM, j * CW + (hh + 1) * HEAD_DIM)
                vh = v[:, hc]
                mu = jnp.mean(vh, axis=-1, keepdims=True)
                vc = vh - mu
                var = jnp.mean(vc * vc, axis=-1, keepdims=True)
                vn = vc * lax.rsqrt(var + EPS) * ln_g_ref[:, gcols] + ln_b_ref[:, gcols]
                sp = jnp.dot(wc[hh], vn.astype(jnp.bfloat16),
                             preferred_element_type=jnp.float32)
                head = j * heads_per_chunk + hh
                sps.append(sp + bst_ref[:, head:head + 1])
            sp = jnp.concatenate(sps, axis=-1)
            bo = u * sp * (zb * _sigmoid(zb))
            mix_ref[rows, d + j * CW:d + (j + 1) * CW] = bo.astype(jnp.bfloat16)

    o_ref[...] += out_proj((n_col_chunks - 1) * CW, d)


def _attn_kernel(x_ref, mem_ref, gx_ref, gm_ref, w_q_hbm, w_kv_hbm, w_xo_hbm, gf_ref,
                 o_ref,
                 w_q_ref, w_kv_ref, w_xo_ref, stage_d_ref, stage_kv_ref, sem_ref,
                 wqk_ref, vxo_ref, p_ref):
    tm, d = x_ref.shape
    m_len = mem_ref.shape[0]
    hd = d // X_HEADS
    bf16 = jnp.bfloat16

    @pl.when(_first_grid_step())
    def _():
        _load_weights_bf16([(w_q_hbm, w_q_ref, stage_d_ref),
                            (w_kv_hbm, w_kv_ref, stage_kv_ref),
                            (w_xo_hbm, w_xo_ref, stage_d_ref)], sem_ref)

    @pl.when(pl.program_id(1) == 0)
    def _():
        m = _rms(mem_ref[...], gm_ref[...]).astype(bf16)
        kv = jnp.dot(m, w_kv_ref[...], preferred_element_type=jnp.float32)
        for hh in range(X_HEADS):
            hc = slice(hh * hd, (hh + 1) * hd)
            mc = slice(hh * m_len, (hh + 1) * m_len)
            k_h = kv[:, hc].astype(bf16)
            v_h = kv[:, d + hh * hd:d + (hh + 1) * hd].astype(bf16)
            wqk = lax.dot_general(w_q_ref[:, hc], k_h, (((1,), (1,)), ((), ())),
                                  preferred_element_type=jnp.float32)
            wqk_ref[:, mc] = (wqk * (1.0 / math.sqrt(hd))).astype(bf16)
            vxo_ref[mc, :] = jnp.dot(v_h, w_xo_ref[hc, :],
                                     preferred_element_type=jnp.float32).astype(bf16)

    halves = [slice(i * ATTN_HALF, (i + 1) * ATTN_HALF) for i in range(tm // ATTN_HALF)]
    hs = [_rms(x_ref[r, :], gx_ref[...]).astype(bf16) for r in halves]
    scores = [jnp.dot(h, wqk_ref[...], preferred_element_type=jnp.float32) for h in hs]
    for hh in range(X_HEADS):
        mc = slice(hh * m_len, (hh + 1) * m_len)
        for s, r in zip(scores, halves):
            p = jnp.exp(s[:, mc] - jnp.max(s[:, mc], axis=-1, keepdims=True))
            l = jnp.sum(p, axis=-1, keepdims=True)
            p_ref[r, mc] = (p * (1.0 / l)).astype(bf16)
    for r in halves:
        y = x_ref[r, :] + jnp.dot(p_ref[r, :], vxo_ref[...],
                                  preferred_element_type=jnp.float32)
        o_ref[r, :] = _rms(y, gf_ref[...])


def _stage(cols, n_stage):
    rows = 1 << int(math.log2(STAGE_BYTES // (4 * cols)))
    return pltpu.VMEM((n_stage, rows, cols), jnp.float32)


def _resident(shape):
    return pl.BlockSpec(shape, lambda *_: (0,) * len(shape),
                        pipeline_mode=pl.Buffered(1))


def _resident_layer(shape):
    return pl.BlockSpec((None,) + tuple(shape[1:]), lambda *_: (0,) * len(shape),
                        pipeline_mode=pl.Buffered(1))


def kernel(x, mem, norm_mix_g, w_in, conv_w, gm_ln_g, gm_ln_b, gm_ws, gm_bs, w_out,
           norm_x_g, norm_mem_g, w_q, w_kv, w_xo, norm_final_g):
    b, s, d = x.shape
    m_len = mem.shape[1]
    assert w_in.shape[0] == 1, "the final norm is fused into the (single) layer's attention call"
    assert s % TM == 0 and TM % CHUNK == 0 and d % CW == 0
    assert s % TM_ATTN == 0 and TM_ATTN % ATTN_HALF == 0
    bf16 = jnp.bfloat16
    f32 = jnp.float32
    params = lambda limit: pltpu.CompilerParams(
        dimension_semantics=("arbitrary", "arbitrary"), vmem_limit_bytes=limit)
    row = lambda a: a.reshape(1, -1)
    hbm = pl.BlockSpec(memory_space=pl.ANY)
    tile = pl.BlockSpec((None, TM, d), lambda i, t: (i, t, 0))
    sems = lambda n: pltpu.SemaphoreType.DMA((n,))

    x = pl.pallas_call(
        _mixer_kernel,
        grid=(b, s // TM),
        in_specs=[tile,
                  _resident((1, d)),
                  hbm,
                  hbm,
                  _resident((1, d)), _resident((1, d)),
                  _resident(gm_ws.shape[1:]),
                  _resident_layer(gm_bs.shape),
                  hbm],
        out_specs=tile,
        out_shape=jax.ShapeDtypeStruct((b, s, d), f32),
        scratch_shapes=[pltpu.VMEM((d, N_SECT * d), bf16),
                        pltpu.VMEM((2 * d, d), bf16),
                        _stage(N_SECT * d, MIX_STAGES), _stage(d, MIX_STAGES),
                        sems(MIX_STAGES),
                        pltpu.VMEM((TM, d), bf16),
                        pltpu.VMEM((2, TM, N_SECT * CW), f32),
                        pltpu.VMEM((SUBLANES, d), f32),
                        pltpu.VMEM((TM, 2 * d), bf16),
                        pltpu.VMEM((CHUNK, gm_bs.shape[1]), f32),
                        pltpu.VMEM((PARAM_ROWS, 1, d), f32),
                        pltpu.SemaphoreType.DMA(())],
        compiler_params=params(MIX_VMEM_LIMIT),
        name="mixer",
    )(x, row(norm_mix_g[0]), w_in[0], conv_w.reshape(CONV_K, 1, d), row(gm_ln_g[0]), row(gm_ln_b[0]),
      gm_ws[0], gm_bs, w_out[0])

    attn_tile = pl.BlockSpec((None, TM_ATTN, d), lambda i, t: (i, t, 0))
    return pl.pallas_call(
        _attn_kernel,
        grid=(b, s // TM_ATTN),
        in_specs=[attn_tile,
                  pl.BlockSpec((None, m_len, d), lambda i, t: (i, 0, 0)),
                  _resident((1, d)), _resident((1, d)),
                  hbm, hbm, hbm,
                  _resident((1, d))],
        out_specs=attn_tile,
        out_shape=jax.ShapeDtypeStruct((b, s, d), f32),
        scratch_shapes=[pltpu.VMEM((d, d), bf16),
                        pltpu.VMEM((d, 2 * d), bf16),
                        pltpu.VMEM((d, d), bf16),
                        _stage(d, ATTN_STAGES), _stage(2 * d, ATTN_STAGES),
                        sems(ATTN_STAGES),
                        pltpu.VMEM((d, X_HEADS * m_len), bf16),
                        pltpu.VMEM((X_HEADS * m_len, d), bf16),
                        pltpu.VMEM((TM_ATTN, X_HEADS * m_len), bf16)],
        compiler_params=params(ATTN_VMEM_LIMIT),
        name="xattn",
    )(x, mem, row(norm_x_g[0]), row(norm_mem_g[0]), w_q[0], w_kv[0], w_xo[0],
      row(norm_final_g))
```

```python
import math

import jax
import jax.numpy as jnp
from jax import lax
from jax.experimental import pallas as pl
from jax.experimental.pallas import tpu as pltpu

EPS = 1e-6
CONV_K = 3
CHUNK = 128
HEAD_DIM = 128
N_SECT = 7
X_HEADS = 4
SUBLANES = 8

TM = 1024
TM_ATTN = 1024
ATTN_HALF = 512
CW = 256
MIX_STAGES = 3
ATTN_STAGES = 3
MIX_SLOT_BYTES = 1 << 21
ATTN_SLOT_BYTES = 1 << 22
MIX_VMEM_LIMIT = 63 * 1024 * 1024
ATTN_VMEM_LIMIT = 63 * 1024 * 1024
PARAM_ROWS = 4


def _rms(x, g):
    ms = jnp.mean(x * x, axis=-1, keepdims=True)
    return x * lax.rsqrt(ms + EPS) * g


def _sigmoid(z):
    return 0.5 * (1.0 + jnp.tanh(0.5 * z))


def _gelu_tanh(x):
    c = math.sqrt(2.0 / math.pi)
    return 0.5 * x * (1.0 + jnp.tanh(c * (x + 0.044715 * (x * x * x))))


def _load_weights_bf16(jobs, sem_ref):
    n_stage = sem_ref.shape[0]
    blocks = [(src, dst, stage, r)
              for src, dst, stage in jobs
              for r in range(0, src.shape[0], stage.shape[1])]

    def copy(k):
        src, _, stage, r = blocks[k]
        return pltpu.make_async_copy(src.at[pl.ds(r, stage.shape[1])],
                                     stage.at[k % n_stage], sem_ref.at[k % n_stage])

    for k in range(min(n_stage - 1, len(blocks))):
        copy(k).start()
    for k, (_, dst, stage, r) in enumerate(blocks):
        if k + n_stage - 1 < len(blocks):
            copy(k + n_stage - 1).start()
        copy(k).wait()
        dst[r:r + stage.shape[1], :] = stage[k % n_stage].astype(jnp.bfloat16)


def _first_grid_step():
    return (pl.program_id(0) == 0) & (pl.program_id(1) == 0)


def _mixer_kernel(x_ref, g_ref, w_in_hbm, conv_w_hbm, ln_g_ref, ln_b_ref,
                  ws_ref, bs_ref, w_out_hbm, o_ref,
                  w_in_ref, w_out_ref, stage_in_ref, stage_out_ref, sem_ref,
                  h_ref, proj_ref, carry_ref, mix_ref, bst_ref, conv_w_ref, conv_sem):
    tm, d = x_ref.shape
    n_col_chunks = d // CW
    n_row_blocks = tm // CHUNK
    heads_per_chunk = CW // HEAD_DIM

    @pl.when(_first_grid_step())
    def _():
        conv_copy = pltpu.make_async_copy(conv_w_hbm, conv_w_ref.at[pl.ds(0, CONV_K)], conv_sem)
        conv_copy.start()
        _load_weights_bf16([(w_in_hbm, w_in_ref, stage_in_ref),
                            (w_out_hbm, w_out_ref, stage_out_ref)], sem_ref)
        conv_copy.wait()

    @pl.when(pl.program_id(1) == 0)
    def _():
        carry_ref[...] = jnp.zeros_like(carry_ref)

    h_ref[...] = _rms(x_ref[...], g_ref[...]).astype(jnp.bfloat16)
    bst_ref[...] = bs_ref[...].T

    tri = (lax.broadcasted_iota(jnp.int32, (CHUNK, CHUNK), 0)
           >= lax.broadcasted_iota(jnp.int32, (CHUNK, CHUNK), 1))

    def in_proj(j):
        for s in range(N_SECT):
            proj_ref[j % 2, :, s * CW:(s + 1) * CW] = jnp.dot(
                h_ref[...], w_in_ref[:, s * d + j * CW:s * d + (j + 1) * CW],
                preferred_element_type=jnp.float32)

    def out_proj(c0, c1):
        part = [jnp.dot(mix_ref[:, o + c0:o + c1], w_out_ref[o + c0:o + c1, :],
                        preferred_element_type=jnp.float32) for o in (0, d)]
        return part[0] + part[1]

    in_proj(0)
    for j in range(n_col_chunks):
        cols = slice(j * CW, (j + 1) * CW)
        if j + 1 < n_col_chunks:
            in_proj(j + 1)
        else:
            o_ref[...] = x_ref[...] + out_proj(0, j * CW)

        def sect(s, rows, j=j):
            return proj_ref[j % 2, rows, s * CW:(s + 1) * CW]

        w0 = conv_w_ref[0, :, cols]
        w1 = conv_w_ref[1, :, cols]
        w2 = conv_w_ref[2, :, cols]
        wc = [jnp.where(tri, ws_ref[j * heads_per_chunk + hh], 0.0).astype(jnp.bfloat16)
              for hh in range(heads_per_chunk)]

        for r in range(n_row_blocks):
            rows = slice(r * CHUNK, (r + 1) * CHUNK)
            g = sect(1, rows) * sect(2, rows)
            if r == 0:
                prev = carry_ref[:, cols]
            else:
                prows = slice(r * CHUNK - SUBLANES, r * CHUNK)
                prev = sect(1, prows) * sect(2, prows)
            if r == n_row_blocks - 1:
                carry_ref[:, cols] = g[CHUNK - SUBLANES:, :]
            ext = jnp.concatenate([prev, g], axis=0)
            g1 = ext[SUBLANES - 1:SUBLANES - 1 + CHUNK, :]
            g2 = ext[SUBLANES - 2:SUBLANES - 2 + CHUNK, :]
            conv = w0 * g2 + w1 * g1 + w2 * g
            za = sect(3, rows)
            a = sect(0, rows) * conv * (za * _sigmoid(za))
            mix_ref[rows, cols] = a.astype(jnp.bfloat16)

            u = _gelu_tanh(sect(4, rows))
            v = _gelu_tanh(sect(5, rows))
            zb = sect(6, rows)
            sps = []
            for hh in range(heads_per_chunk):
                hc = slice(hh * HEAD_DIM, (hh + 1) * HEAD_DIM)
                gcols = slice(j * CW + hh * HEAD_DIM, j * CW + (hh + 1) * HEAD_DIM)
                vh = v[:, hc]
                mu = jnp.mean(vh, axis=-1, keepdims=True)
                vc = vh - mu
                var = jnp.mean(vc * vc, axis=-1, keepdims=True)
                vn = vc * lax.rsqrt(var + EPS) * ln_g_ref[:, gcols] + ln_b_ref[:, gcols]
                sp = jnp.dot(wc[hh], vn.astype(jnp.bfloat16),
                             preferred_element_type=jnp.float32)
                head = j * heads_per_chunk + hh
                sps.append(sp + bst_ref[:, head:head + 1])
            sp = jnp.concatenate(sps, axis=-1)
            bo = u * sp * (zb * _sigmoid(zb))
            mix_ref[rows, d + j * CW:d + (j + 1) * CW] = bo.astype(jnp.bfloat16)

    o_ref[...] += out_proj((n_col_chunks - 1) * CW, d)


def _attn_kernel(x_ref, mem_ref, gx_ref, gm_ref, w_q_hbm, w_kv_hbm, w_xo_hbm, gf_ref,
                 o_ref,
                 w_q_ref, w_kv_ref, w_xo_ref, stage_d_ref, stage_kv_ref, sem_ref,
                 wqk_ref, vxo_ref, p_ref):
    tm, d = x_ref.shape
    m_len = mem_ref.shape[0]
    hd = d // X_HEADS
    bf16 = jnp.bfloat16

    @pl.when(_first_grid_step())
    def _():
        _load_weights_bf16([(w_q_hbm, w_q_ref, stage_d_ref),
                            (w_kv_hbm, w_kv_ref, stage_kv_ref),
                            (w_xo_hbm, w_xo_ref, stage_d_ref)], sem_ref)

    @pl.when(pl.program_id(1) == 0)
    def _():
        m = _rms(mem_ref[...], gm_ref[...]).astype(bf16)
        kv = jnp.dot(m, w_kv_ref[...], preferred_element_type=jnp.float32)
        for hh in range(X_HEADS):
            hc = slice(hh * hd, (hh + 1) * hd)
            mc = slice(hh * m_len, (hh + 1) * m_len)
            k_h = kv[:, hc].astype(bf16)
            v_h = kv[:, d + hh * hd:d + (hh + 1) * hd].astype(bf16)
            wqk = lax.dot_general(w_q_ref[:, hc], k_h, (((1,), (1,)), ((), ())),
                                  preferred_element_type=jnp.float32)
            wqk_ref[:, mc] = (wqk * (1.0 / math.sqrt(hd))).astype(bf16)
            vxo_ref[mc, :] = jnp.dot(v_h, w_xo_ref[hc, :],
                                     preferred_element_type=jnp.float32).astype(bf16)

    halves = [slice(i * ATTN_HALF, (i + 1) * ATTN_HALF) for i in range(tm // ATTN_HALF)]
    hs = [_rms(x_ref[r, :], gx_ref[...]).astype(bf16) for r in halves]
    scores = [jnp.dot(h, wqk_ref[...], preferred_element_type=jnp.float32) for h in hs]
    for hh in range(X_HEADS):
        mc = slice(hh * m_len, (hh + 1) * m_len)
        for s, r in zip(scores, halves):
            p = jnp.exp(s[:, mc] - jnp.max(s[:, mc], axis=-1, keepdims=True))
            l = jnp.sum(p, axis=-1, keepdims=True)
            p_ref[r, mc] = (p * (1.0 / l)).astype(bf16)
    for r in halves:
        y = x_ref[r, :] + jnp.dot(p_ref[r, :], vxo_ref[...],
                                  preferred_element_type=jnp.float32)
        o_ref[r, :] = _rms(y, gf_ref[...])


def _stage(cols, n_stage, slot_bytes):
    rows = 1 << int(math.log2(slot_bytes // (4 * cols)))
    return pltpu.VMEM((n_stage, rows, cols), jnp.float32)


def _resident(shape):
    return pl.BlockSpec(shape, lambda *_: (0,) * len(shape),
                        pipeline_mode=pl.Buffered(1))


def _resident_layer(shape):
    return pl.BlockSpec((None,) + tuple(shape[1:]), lambda *_: (0,) * len(shape),
                        pipeline_mode=pl.Buffered(1))


def kernel(x, mem, norm_mix_g, w_in, conv_w, gm_ln_g, gm_ln_b, gm_ws, gm_bs, w_out,
           norm_x_g, norm_mem_g, w_q, w_kv, w_xo, norm_final_g):
    b, s, d = x.shape
    m_len = mem.shape[1]
    assert w_in.shape[0] == 1, "the final norm is fused into the (single) layer's attention call"
    assert s % TM == 0 and TM % CHUNK == 0 and d % CW == 0
    assert s % TM_ATTN == 0 and TM_ATTN % ATTN_HALF == 0
    bf16 = jnp.bfloat16
    f32 = jnp.float32
    params = lambda limit: pltpu.CompilerParams(
        dimension_semantics=("arbitrary", "arbitrary"), vmem_limit_bytes=limit)
    row = lambda a: a.reshape(1, -1)
    hbm = pl.BlockSpec(memory_space=pl.ANY)
    tile = pl.BlockSpec((None, TM, d), lambda i, t: (i, t, 0))
    sems = lambda n: pltpu.SemaphoreType.DMA((n,))

    x = pl.pallas_call(
        _mixer_kernel,
        grid=(b, s // TM),
        in_specs=[tile,
                  _resident((1, d)),
                  hbm,
                  hbm,
                  _resident((1, d)), _resident((1, d)),
                  _resident(gm_ws.shape[1:]),
                  _resident_layer(gm_bs.shape),
                  hbm],
        out_specs=tile,
        out_shape=jax.ShapeDtypeStruct((b, s, d), f32),
        scratch_shapes=[pltpu.VMEM((d, N_SECT * d), bf16),
                        pltpu.VMEM((2 * d, d), bf16),
                        _stage(N_SECT * d, MIX_STAGES, MIX_SLOT_BYTES),
                        _stage(d, MIX_STAGES, MIX_SLOT_BYTES),
                        sems(MIX_STAGES),
                        pltpu.VMEM((TM, d), bf16),
                        pltpu.VMEM((2, TM, N_SECT * CW), f32),
                        pltpu.VMEM((SUBLANES, d), f32),
                        pltpu.VMEM((TM, 2 * d), bf16),
                        pltpu.VMEM((CHUNK, gm_bs.shape[1]), f32),
                        pltpu.VMEM((PARAM_ROWS, 1, d), f32),
                        pltpu.SemaphoreType.DMA(())],
        compiler_params=params(MIX_VMEM_LIMIT),
        name="mixer",
    )(x, row(norm_mix_g[0]), w_in[0], conv_w.reshape(CONV_K, 1, d), row(gm_ln_g[0]), row(gm_ln_b[0]),
      gm_ws[0], gm_bs, w_out[0])

    attn_tile = pl.BlockSpec((None, TM_ATTN, d), lambda i, t: (i, t, 0))
    return pl.pallas_call(
        _attn_kernel,
        grid=(b, s // TM_ATTN),
        in_specs=[attn_tile,
                  pl.BlockSpec((None, m_len, d), lambda i, t: (i, 0, 0)),
                  _resident((1, d)), _resident((1, d)),
                  hbm, hbm, hbm,
                  _resident((1, d))],
        out_specs=attn_tile,
        out_shape=jax.ShapeDtypeStruct((b, s, d), f32),
        scratch_shapes=[pltpu.VMEM((d, d), bf16),
                        pltpu.VMEM((d, 2 * d), bf16),
                        pltpu.VMEM((d, d), bf16),
                        _stage(d, ATTN_STAGES, ATTN_SLOT_BYTES),
                        _stage(2 * d, ATTN_STAGES, ATTN_SLOT_BYTES),
                        sems(ATTN_STAGES),
                        pltpu.VMEM((d, X_HEADS * m_len), bf16),
                        pltpu.VMEM((X_HEADS * m_len, d), bf16),
                        pltpu.VMEM((TM_ATTN, X_HEADS * m_len), bf16)],
        compiler_params=params(ATTN_VMEM_LIMIT),
        name="xattn",
    )(x, mem, row(norm_x_g[0]), row(norm_mem_g[0]), w_q[0], w_kv[0], w_xo[0],
      row(norm_final_g))
```

```python
import math

import jax
import jax.numpy as jnp
from jax import lax
from jax.experimental import pallas as pl
from jax.experimental.pallas import tpu as pltpu

EPS = 1e-6
CONV_K = 3
CHUNK = 128
HEAD_DIM = 128
N_SECT = 7
X_HEADS = 4
SUBLANES = 8

TM = 1024
TM_ATTN = 1024
ATTN_HALF = 512
CW = 256
MIX_STAGES = 3
ATTN_STAGES = 3
STAGE_BYTES = 1 << 21
MIX_VMEM_LIMIT = 63 * 1024 * 1024
ATTN_VMEM_LIMIT = 56 * 1024 * 1024
PARAM_ROWS = 4


def _rms(x, g):
    ms = jnp.mean(x * x, axis=-1, keepdims=True)
    return x * lax.rsqrt(ms + EPS) * g


def _sigmoid(z):
    return 0.5 * (1.0 + jnp.tanh(0.5 * z))


def _gelu_tanh(x):
    c = math.sqrt(2.0 / math.pi)
    return 0.5 * x * (1.0 + jnp.tanh(c * (x + 0.044715 * (x * x * x))))


def _load_weights_bf16(jobs, sem_ref):
    n_stage = sem_ref.shape[0]
    blocks = [(src, dst, stage, r, scale)
              for src, dst, stage, scale in jobs
              for r in range(0, src.shape[0], stage.shape[1])]

    def copy(k):
        src, _, stage, r, _ = blocks[k]
        return pltpu.make_async_copy(src.at[pl.ds(r, stage.shape[1])],
                                     stage.at[k % n_stage], sem_ref.at[k % n_stage])

    for k in range(min(n_stage - 1, len(blocks))):
        copy(k).start()
    for k, (_, dst, stage, r, scale) in enumerate(blocks):
        if k + n_stage - 1 < len(blocks):
            copy(k + n_stage - 1).start()
        copy(k).wait()
        slab = stage[k % n_stage]
        if scale is not None:
            slab = slab * scale[r:r + stage.shape[1], :]
        dst[r:r + stage.shape[1], :] = slab.astype(jnp.bfloat16)


def _first_grid_step():
    return (pl.program_id(0) == 0) & (pl.program_id(1) == 0)


def _mixer_kernel(x_ref, g_ref, w_in_hbm, conv_w_hbm, ln_g_ref, ln_b_ref,
                  ws_ref, bs_ref, w_out_hbm, o_ref,
                  w_in_ref, w_out_ref, stage_in_ref, stage_out_ref, sem_ref,
                  h_ref, proj_ref, carry_ref, mix_ref, bst_ref, conv_w_ref, conv_sem):
    tm, d = x_ref.shape
    n_col_chunks = d // CW
    n_row_blocks = tm // CHUNK
    heads_per_chunk = CW // HEAD_DIM

    @pl.when(_first_grid_step())
    def _():
        conv_copy = pltpu.make_async_copy(conv_w_hbm, conv_w_ref.at[pl.ds(0, CONV_K)], conv_sem)
        conv_copy.start()
        _load_weights_bf16([(w_in_hbm, w_in_ref, stage_in_ref, None),
                            (w_out_hbm, w_out_ref, stage_out_ref, None)], sem_ref)
        conv_copy.wait()

    @pl.when(pl.program_id(1) == 0)
    def _():
        carry_ref[...] = jnp.zeros_like(carry_ref)

    h_ref[...] = _rms(x_ref[...], g_ref[...]).astype(jnp.bfloat16)
    bst_ref[...] = bs_ref[...].T

    tri = (lax.broadcasted_iota(jnp.int32, (CHUNK, CHUNK), 0)
           >= lax.broadcasted_iota(jnp.int32, (CHUNK, CHUNK), 1))

    def in_proj(j):
        for s in range(N_SECT):
            proj_ref[j % 2, :, s * CW:(s + 1) * CW] = jnp.dot(
                h_ref[...], w_in_ref[:, s * d + j * CW:s * d + (j + 1) * CW],
                preferred_element_type=jnp.float32)

    def out_proj(c0, c1):
        part = [jnp.dot(mix_ref[:, o + c0:o + c1], w_out_ref[o + c0:o + c1, :],
                        preferred_element_type=jnp.float32) for o in (0, d)]
        return part[0] + part[1]

    in_proj(0)
    for j in range(n_col_chunks):
        cols = slice(j * CW, (j + 1) * CW)
        if j + 1 < n_col_chunks:
            in_proj(j + 1)
        else:
            o_ref[...] = x_ref[...] + out_proj(0, j * CW)

        def sect(s, rows, j=j):
            return proj_ref[j % 2, rows, s * CW:(s + 1) * CW]

        w0 = conv_w_ref[0, :, cols]
        w1 = conv_w_ref[1, :, cols]
        w2 = conv_w_ref[2, :, cols]
        wc = [jnp.where(tri, ws_ref[j * heads_per_chunk + hh], 0.0).astype(jnp.bfloat16)
              for hh in range(heads_per_chunk)]

        for r in range(n_row_blocks):
            rows = slice(r * CHUNK, (r + 1) * CHUNK)
            g = sect(1, rows) * sect(2, rows)
            if r == 0:
                prev = carry_ref[:, cols]
            else:
                prows = slice(r * CHUNK - SUBLANES, r * CHUNK)
                prev = sect(1, prows) * sect(2, prows)
            if r == n_row_blocks - 1:
                carry_ref[:, cols] = g[CHUNK - SUBLANES:, :]
            ext = jnp.concatenate([prev, g], axis=0)
            g1 = ext[SUBLANES - 1:SUBLANES - 1 + CHUNK, :]
            g2 = ext[SUBLANES - 2:SUBLANES - 2 + CHUNK, :]
            conv = w0 * g2 + w1 * g1 + w2 * g
            za = sect(3, rows)
            a = sect(0, rows) * conv * (za * _sigmoid(za))
            mix_ref[rows, cols] = a.astype(jnp.bfloat16)

            u = _gelu_tanh(sect(4, rows))
            v = _gelu_tanh(sect(5, rows))
            zb = sect(6, rows)
            sps = []
            for hh in range(heads_per_chunk):
                hc = slice(hh * HEAD_DIM, (hh + 1) * HEAD_DIM)
                gcols = slice(j * CW + hh * HEAD_DIM, j * CW + (hh + 1) * HEAD_DIM)
                vh = v[:, hc]
                mu = jnp.mean(vh, axis=-1, keepdims=True)
                vc = vh - mu
                var = jnp.mean(vc * vc, axis=-1, keepdims=True)
                vn = vc * lax.rsqrt(var + EPS) * ln_g_ref[:, gcols] + ln_b_ref[:, gcols]
                sp = jnp.dot(wc[hh], vn.astype(jnp.bfloat16),
                             preferred_element_type=jnp.float32)
                head = j * heads_per_chunk + hh
                sps.append(sp + bst_ref[:, head:head + 1])
            sp = jnp.concatenate(sps, axis=-1)
            bo = u * sp * (zb * _sigmoid(zb))
            mix_ref[rows, d + j * CW:d + (j + 1) * CW] = bo.astype(jnp.bfloat16)

    o_ref[...] += out_proj((n_col_chunks - 1) * CW, d)


def _attn_kernel(x_ref, mem_ref, gx_ref, gm_ref, w_q_hbm, w_kv_hbm, w_xo_hbm, gf_ref,
                 o_ref,
                 w_q_ref, w_kv_ref, w_xo_ref, stage_d_ref, stage_kv_ref, sem_ref,
                 wqk_ref, vxo_ref, p_ref):
    tm, d = x_ref.shape
    m_len = mem_ref.shape[0]
    hd = d // X_HEADS
    bf16 = jnp.bfloat16

    @pl.when(_first_grid_step())
    def _():
        g_col = jnp.broadcast_to(gx_ref[...], (SUBLANES, d)).T[:, 0:1]
        _load_weights_bf16([(w_q_hbm, w_q_ref, stage_d_ref, g_col),
                            (w_kv_hbm, w_kv_ref, stage_kv_ref, None),
                            (w_xo_hbm, w_xo_ref, stage_d_ref, None)], sem_ref)

    @pl.when(pl.program_id(1) == 0)
    def _():
        m = _rms(mem_ref[...], gm_ref[...]).astype(bf16)
        kv = jnp.dot(m, w_kv_ref[...], preferred_element_type=jnp.float32)
        for hh in range(X_HEADS):
            hc = slice(hh * hd, (hh + 1) * hd)
            mc = slice(hh * m_len, (hh + 1) * m_len)
            k_h = kv[:, hc].astype(bf16)
            v_h = kv[:, d + hh * hd:d + (hh + 1) * hd].astype(bf16)
            wqk = lax.dot_general(w_q_ref[:, hc], k_h, (((1,), (1,)), ((), ())),
                                  preferred_element_type=jnp.float32)
            wqk_ref[:, mc] = (wqk * (1.0 / math.sqrt(hd))).astype(bf16)
            vxo_ref[mc, :] = jnp.dot(v_h, w_xo_ref[hc, :],
                                     preferred_element_type=jnp.float32).astype(bf16)

    halves = [slice(i * ATTN_HALF, (i + 1) * ATTN_HALF) for i in range(tm // ATTN_HALF)]
    xs = [x_ref[r, :] for r in halves]
    scores = [jnp.dot(x.astype(bf16), wqk_ref[...], preferred_element_type=jnp.float32)
              for x in xs]
    rinv = [lax.rsqrt(jnp.mean(x * x, axis=-1, keepdims=True) + EPS) for x in xs]
    for hh in range(X_HEADS):
        mc = slice(hh * m_len, (hh + 1) * m_len)
        for s, ri, r in zip(scores, rinv, halves):
            p = jnp.exp(ri * (s[:, mc] - jnp.max(s[:, mc], axis=-1, keepdims=True)))
            l = jnp.sum(p, axis=-1, keepdims=True)
            p_ref[r, mc] = (p * (1.0 / l)).astype(bf16)
    for r in halves:
        y = x_ref[r, :] + jnp.dot(p_ref[r, :], vxo_ref[...],
                                  preferred_element_type=jnp.float32)
        o_ref[r, :] = _rms(y, gf_ref[...])


def _stage(cols, n_stage):
    rows = 1 << int(math.log2(STAGE_BYTES // (4 * cols)))
    return pltpu.VMEM((n_stage, rows, cols), jnp.float32)


def _resident(shape):
    return pl.BlockSpec(shape, lambda *_: (0,) * len(shape),
                        pipeline_mode=pl.Buffered(1))


def _resident_layer(shape):
    return pl.BlockSpec((None,) + tuple(shape[1:]), lambda *_: (0,) * len(shape),
                        pipeline_mode=pl.Buffered(1))


def kernel(x, mem, norm_mix_g, w_in, conv_w, gm_ln_g, gm_ln_b, gm_ws, gm_bs, w_out,
           norm_x_g, norm_mem_g, w_q, w_kv, w_xo, norm_final_g):
    b, s, d = x.shape
    m_len = mem.shape[1]
    assert w_in.shape[0] == 1, "the final norm is fused into the (single) layer's attention call"
    assert s % TM == 0 and TM % CHUNK == 0 and d % CW == 0
    assert s % TM_ATTN == 0 and TM_ATTN % ATTN_HALF == 0
    bf16 = jnp.bfloat16
    f32 = jnp.float32
    params = lambda limit: pltpu.CompilerParams(
        dimension_semantics=("arbitrary", "arbitrary"), vmem_limit_bytes=limit)
    row = lambda a: a.reshape(1, -1)
    hbm = pl.BlockSpec(memory_space=pl.ANY)
    tile = pl.BlockSpec((None, TM, d), lambda i, t: (i, t, 0))
    sems = lambda n: pltpu.SemaphoreType.DMA((n,))

    x = pl.pallas_call(
        _mixer_kernel,
        grid=(b, s // TM),
        in_specs=[tile,
                  _resident((1, d)),
                  hbm,
                  hbm,
                  _resident((1, d)), _resident((1, d)),
                  _resident(gm_ws.shape[1:]),
                  _resident_layer(gm_bs.shape),
                  hbm],
        out_specs=tile,
        out_shape=jax.ShapeDtypeStruct((b, s, d), f32),
        scratch_shapes=[pltpu.VMEM((d, N_SECT * d), bf16),
                        pltpu.VMEM((2 * d, d), bf16),
                        _stage(N_SECT * d, MIX_STAGES), _stage(d, MIX_STAGES),
                        sems(MIX_STAGES),
                        pltpu.VMEM((TM, d), bf16),
                        pltpu.VMEM((2, TM, N_SECT * CW), f32),
                        pltpu.VMEM((SUBLANES, d), f32),
                        pltpu.VMEM((TM, 2 * d), bf16),
                        pltpu.VMEM((CHUNK, gm_bs.shape[1]), f32),
                        pltpu.VMEM((PARAM_ROWS, 1, d), f32),
                        pltpu.SemaphoreType.DMA(())],
        compiler_params=params(MIX_VMEM_LIMIT),
        name="mixer",
    )(x, row(norm_mix_g[0]), w_in[0], conv_w.reshape(CONV_K, 1, d), row(gm_ln_g[0]), row(gm_ln_b[0]),
      gm_ws[0], gm_bs, w_out[0])

    attn_tile = pl.BlockSpec((None, TM_ATTN, d), lambda i, t: (i, t, 0))
    return pl.pallas_call(
        _attn_kernel,
        grid=(b, s // TM_ATTN),
        in_specs=[attn_tile,
                  pl.BlockSpec((None, m_len, d), lambda i, t: (i, 0, 0)),
                  _resident((1, d)), _resident((1, d)),
                  hbm, hbm, hbm,
                  _resident((1, d))],
        out_specs=attn_tile,
        out_shape=jax.ShapeDtypeStruct((b, s, d), f32),
        scratch_shapes=[pltpu.VMEM((d, d), bf16),
                        pltpu.VMEM((d, 2 * d), bf16),
                        pltpu.VMEM((d, d), bf16),
                        _stage(d, ATTN_STAGES), _stage(2 * d, ATTN_STAGES),
                        sems(ATTN_STAGES),
                        pltpu.VMEM((d, X_HEADS * m_len), bf16),
                        pltpu.VMEM((X_HEADS * m_len, d), bf16),
                        pltpu.VMEM((TM_ATTN, X_HEADS * m_len), bf16)],
        compiler_params=params(ATTN_VMEM_LIMIT),
        name="xattn",
    )(x, mem, row(norm_x_g[0]), row(norm_mem_g[0]), w_q[0], w_kv[0], w_xo[0],
      row(norm_final_g))
```

```python
import math

import jax
import jax.numpy as jnp
from jax import lax
from jax.experimental import pallas as pl
from jax.experimental.pallas import tpu as pltpu

EPS = 1e-6
CONV_K = 3
CHUNK = 128
HEAD_DIM = 128
N_SECT = 7
X_HEADS = 4
SUBLANES = 8

TM = 1024
TM_ATTN = 1024
ATTN_HALF = 512
CW = 256
MIX_STAGES = 3
ATTN_STAGES = 3
STAGE_BYTES = 1 << 21
MIX_VMEM_LIMIT = 63 * 1024 * 1024
ATTN_VMEM_LIMIT = 56 * 1024 * 1024
PARAM_ROWS = 4


def _rms(x, g):
    ms = jnp.mean(x * x, axis=-1, keepdims=True)
    return x * lax.rsqrt(ms + EPS) * g


def _sigmoid(z):
    return 0.5 * (1.0 + jnp.tanh(0.5 * z))


def _gelu_tanh(x):
    c = math.sqrt(2.0 / math.pi)
    return 0.5 * x * (1.0 + jnp.tanh(c * (x + 0.044715 * (x * x * x))))


def _load_weights_bf16(jobs, sem_ref):
    n_stage = sem_ref.shape[0]
    blocks = [(src, dst, stage, r, scale)
              for src, dst, stage, scale in jobs
              for r in range(0, src.shape[0], stage.shape[1])]

    def copy(k):
        src, _, stage, r, _ = blocks[k]
        return pltpu.make_async_copy(src.at[pl.ds(r, stage.shape[1])],
                                     stage.at[k % n_stage], sem_ref.at[k % n_stage])

    for k in range(min(n_stage - 1, len(blocks))):
        copy(k).start()
    for k, (_, dst, stage, r, scale) in enumerate(blocks):
        if k + n_stage - 1 < len(blocks):
            copy(k + n_stage - 1).start()
        copy(k).wait()
        slab = stage[k % n_stage]
        if scale is not None:
            slab = slab * scale[r:r + stage.shape[1], :]
        dst[r:r + stage.shape[1], :] = slab.astype(jnp.bfloat16)


def _first_grid_step():
    return (pl.program_id(0) == 0) & (pl.program_id(1) == 0)


def _mixer_kernel(x_ref, g_ref, w_in_hbm, conv_w_hbm, ln_g_ref, ln_b_ref,
                  ws_ref, bs_ref, w_out_hbm, o_ref,
                  w_in_ref, w_out_ref, stage_in_ref, stage_out_ref, sem_ref,
                  h_ref, proj_ref, carry_ref, mix_ref, bst_ref, conv_w_ref, conv_sem):
    tm, d = x_ref.shape
    n_col_chunks = d // CW
    n_row_blocks = tm // CHUNK
    heads_per_chunk = CW // HEAD_DIM

    @pl.when(_first_grid_step())
    def _():
        conv_copy = pltpu.make_async_copy(conv_w_hbm, conv_w_ref.at[pl.ds(0, CONV_K)], conv_sem)
        conv_copy.start()
        _load_weights_bf16([(w_in_hbm, w_in_ref, stage_in_ref, None),
                            (w_out_hbm, w_out_ref, stage_out_ref, None)], sem_ref)
        conv_copy.wait()

    @pl.when(pl.program_id(1) == 0)
    def _():
        carry_ref[...] = jnp.zeros_like(carry_ref)

    h_ref[...] = _rms(x_ref[...], g_ref[...]).astype(jnp.bfloat16)
    bst_ref[...] = bs_ref[...].T

    tri = (lax.broadcasted_iota(jnp.int32, (CHUNK, CHUNK), 0)
           >= lax.broadcasted_iota(jnp.int32, (CHUNK, CHUNK), 1))

    def in_proj(j):
        for s in range(N_SECT):
            proj_ref[j % 2, :, s * CW:(s + 1) * CW] = jnp.dot(
                h_ref[...], w_in_ref[:, s * d + j * CW:s * d + (j + 1) * CW],
                preferred_element_type=jnp.float32)

    def out_proj(c0, c1):
        part = [jnp.dot(mix_ref[:, o + c0:o + c1], w_out_ref[o + c0:o + c1, :],
                        preferred_element_type=jnp.float32) for o in (0, d)]
        return part[0] + part[1]

    in_proj(0)
    for j in range(n_col_chunks):
        cols = slice(j * CW, (j + 1) * CW)
        if j + 1 < n_col_chunks:
            in_proj(j + 1)
        else:
            o_ref[...] = x_ref[...] + out_proj(0, j * CW)

        def sect(s, rows, j=j):
            return proj_ref[j % 2, rows, s * CW:(s + 1) * CW]

        w0 = conv_w_ref[0, :, cols]
        w1 = conv_w_ref[1, :, cols]
        w2 = conv_w_ref[2, :, cols]
        wc = [jnp.where(tri, ws_ref[j * heads_per_chunk + hh], 0.0).astype(jnp.bfloat16)
              for hh in range(heads_per_chunk)]

        for r in range(n_row_blocks):
            rows = slice(r * CHUNK, (r + 1) * CHUNK)
            g = sect(1, rows) * sect(2, rows)
            if r == 0:
                prev = carry_ref[:, cols]
            else:
                prows = slice(r * CHUNK - SUBLANES, r * CHUNK)
                prev = sect(1, prows) * sect(2, prows)
            if r == n_row_blocks - 1:
                carry_ref[:, cols] = g[CHUNK - SUBLANES:, :]
            ext = jnp.concatenate([prev, g], axis=0)
            g1 = ext[SUBLANES - 1:SUBLANES - 1 + CHUNK, :]
            g2 = ext[SUBLANES - 2:SUBLANES - 2 + CHUNK, :]
            conv = w0 * g2 + w1 * g1 + w2 * g
            za = sect(3, rows)
            a = sect(0, rows) * conv * (za * _sigmoid(za))
            mix_ref[rows, cols] = a.astype(jnp.bfloat16)

            u = _gelu_tanh(sect(4, rows))
            v = _gelu_tanh(sect(5, rows))
            zb = sect(6, rows)
            sps = []
            for hh in range(heads_per_chunk):
                hc = slice(hh * HEAD_DIM, (hh + 1) * HEAD_DIM)
                gcols = slice(j * CW + hh * HEAD_DIM, j * CW + (hh + 1) * HEAD_DIM)
                vh = v[:, hc]
                mu = jnp.mean(vh, axis=-1, keepdims=True)
                vc = vh - mu
                var = jnp.mean(vc * vc, axis=-1, keepdims=True)
                vn = vc * lax.rsqrt(var + EPS) * ln_g_ref[:, gcols] + ln_b_ref[:, gcols]
                sp = jnp.dot(wc[hh], vn.astype(jnp.bfloat16),
                             preferred_element_type=jnp.float32)
                head = j * heads_per_chunk + hh
                sps.append(sp + bst_ref[:, head:head + 1])
            sp = jnp.concatenate(sps, axis=-1)
            bo = u * sp * (zb * _sigmoid(zb))
            mix_ref[rows, d + j * CW:d + (j + 1) * CW] = bo.astype(jnp.bfloat16)

    o_ref[...] += out_proj((n_col_chunks - 1) * CW, d)


def _attn_kernel(x_ref, mem_ref, gx_ref, gm_ref, w_q_hbm, w_kv_hbm, w_xo_hbm, gf_ref,
                 o_ref,
                 w_q_ref, w_kv_ref, w_xo_ref, stage_d_ref, stage_kv_ref, sem_ref,
                 wqk_ref, vxo_ref, p_ref):
    tm, d = x_ref.shape
    m_len = mem_ref.shape[0]
    hd = d // X_HEADS
    bf16 = jnp.bfloat16

    @pl.when(_first_grid_step())
    def _():
        g_col = jnp.broadcast_to(gx_ref[...], (SUBLANES, d)).T[:, 0:1]
        _load_weights_bf16([(w_q_hbm, w_q_ref, stage_d_ref, g_col),
                            (w_kv_hbm, w_kv_ref, stage_kv_ref, None),
                            (w_xo_hbm, w_xo_ref, stage_d_ref, None)], sem_ref)

    @pl.when(pl.program_id(1) == 0)
    def _():
        m = _rms(mem_ref[...], gm_ref[...]).astype(bf16)
        kv = jnp.dot(m, w_kv_ref[...], preferred_element_type=jnp.float32)
        for hh in range(X_HEADS):
            hc = slice(hh * hd, (hh + 1) * hd)
            mc = slice(hh * m_len, (hh + 1) * m_len)
            k_h = kv[:, hc].astype(bf16)
            v_h = kv[:, d + hh * hd:d + (hh + 1) * hd].astype(bf16)
            wqk = lax.dot_general(w_q_ref[:, hc], k_h, (((1,), (1,)), ((), ())),
                                  preferred_element_type=jnp.float32)
            wqk_ref[:, mc] = (wqk * (1.0 / math.sqrt(hd))).astype(bf16)
            vxo_ref[mc, :] = jnp.dot(v_h, w_xo_ref[hc, :],
                                     preferred_element_type=jnp.float32).astype(bf16)

    halves = [slice(i * ATTN_HALF, (i + 1) * ATTN_HALF) for i in range(tm // ATTN_HALF)]
    xs = [x_ref[r, :] for r in halves]
    scores = [jnp.dot(x.astype(bf16), wqk_ref[...], preferred_element_type=jnp.float32)
              for x in xs]
    rinv = [lax.rsqrt(jnp.mean(x * x, axis=-1, keepdims=True) + EPS) for x in xs]
    for hh in range(X_HEADS):
        mc = slice(hh * m_len, (hh + 1) * m_len)
        for s, ri, r in zip(scores, rinv, halves):
            z = s[:, mc] - jnp.max(s[:, mc], axis=-1, keepdims=True)
            p = jnp.exp(jnp.where(ri > 0.0, ri * z, 0.0))
            l = jnp.sum(p, axis=-1, keepdims=True)
            p_ref[r, mc] = (p * (1.0 / l)).astype(bf16)
    for r in halves:
        y = x_ref[r, :] + jnp.dot(p_ref[r, :], vxo_ref[...],
                                  preferred_element_type=jnp.float32)
        o_ref[r, :] = _rms(y, gf_ref[...])


def _stage(cols, n_stage):
    rows = 1 << int(math.log2(STAGE_BYTES // (4 * cols)))
    return pltpu.VMEM((n_stage, rows, cols), jnp.float32)


def _resident(shape):
    return pl.BlockSpec(shape, lambda *_: (0,) * len(shape),
                        pipeline_mode=pl.Buffered(1))


def _resident_layer(shape):
    return pl.BlockSpec((None,) + tuple(shape[1:]), lambda *_: (0,) * len(shape),
                        pipeline_mode=pl.Buffered(1))


def kernel(x, mem, norm_mix_g, w_in, conv_w, gm_ln_g, gm_ln_b, gm_ws, gm_bs, w_out,
           norm_x_g, norm_mem_g, w_q, w_kv, w_xo, norm_final_g):
    b, s, d = x.shape
    m_len = mem.shape[1]
    assert w_in.shape[0] == 1, "the final norm is fused into the (single) layer's attention call"
    assert s % TM == 0 and TM % CHUNK == 0 and d % CW == 0
    assert s % TM_ATTN == 0 and TM_ATTN % ATTN_HALF == 0
    bf16 = jnp.bfloat16
    f32 = jnp.float32
    params = lambda limit: pltpu.CompilerParams(
        dimension_semantics=("arbitrary", "arbitrary"), vmem_limit_bytes=limit)
    row = lambda a: a.reshape(1, -1)
    hbm = pl.BlockSpec(memory_space=pl.ANY)
    tile = pl.BlockSpec((None, TM, d), lambda i, t: (i, t, 0))
    sems = lambda n: pltpu.SemaphoreType.DMA((n,))

    x = pl.pallas_call(
        _mixer_kernel,
        grid=(b, s // TM),
        in_specs=[tile,
                  _resident((1, d)),
                  hbm,
                  hbm,
                  _resident((1, d)), _resident((1, d)),
                  _resident(gm_ws.shape[1:]),
                  _resident_layer(gm_bs.shape),
                  hbm],
        out_specs=tile,
        out_shape=jax.ShapeDtypeStruct((b, s, d), f32),
        scratch_shapes=[pltpu.VMEM((d, N_SECT * d), bf16),
                        pltpu.VMEM((2 * d, d), bf16),
                        _stage(N_SECT * d, MIX_STAGES), _stage(d, MIX_STAGES),
                        sems(MIX_STAGES),
                        pltpu.VMEM((TM, d), bf16),
                        pltpu.VMEM((2, TM, N_SECT * CW), f32),
                        pltpu.VMEM((SUBLANES, d), f32),
                        pltpu.VMEM((TM, 2 * d), bf16),
                        pltpu.VMEM((CHUNK, gm_bs.shape[1]), f32),
                        pltpu.VMEM((PARAM_ROWS, 1, d), f32),
                        pltpu.SemaphoreType.DMA(())],
        compiler_params=params(MIX_VMEM_LIMIT),
        name="mixer",
    )(x, row(norm_mix_g[0]), w_in[0], conv_w.reshape(CONV_K, 1, d), row(gm_ln_g[0]), row(gm_ln_b[0]),
      gm_ws[0], gm_bs, w_out[0])

    attn_tile = pl.BlockSpec((None, TM_ATTN, d), lambda i, t: (i, t, 0))
    return pl.pallas_call(
        _attn_kernel,
        grid=(b, s // TM_ATTN),
        in_specs=[attn_tile,
                  pl.BlockSpec((None, m_len, d), lambda i, t: (i, 0, 0)),
                  _resident((1, d)), _resident((1, d)),
                  hbm, hbm, hbm,
                  _resident((1, d))],
        out_specs=attn_tile,
        out_shape=jax.ShapeDtypeStruct((b, s, d), f32),
        scratch_shapes=[pltpu.VMEM((d, d), bf16),
                        pltpu.VMEM((d, 2 * d), bf16),
                        pltpu.VMEM((d, d), bf16),
                        _stage(d, ATTN_STAGES), _stage(2 * d, ATTN_STAGES),
                        sems(ATTN_STAGES),
                        pltpu.VMEM((d, X_HEADS * m_len), bf16),
                        pltpu.VMEM((X_HEADS * m_len, d), bf16),
                        pltpu.VMEM((TM_ATTN, X_HEADS * m_len), bf16)],
        compiler_params=params(ATTN_VMEM_LIMIT),
        name="xattn",
    )(x, mem, row(norm_x_g[0]), row(norm_mem_g[0]), w_q[0], w_kv[0], w_xo[0],
      row(norm_final_g))
```

```python
import math

import jax
import jax.numpy as jnp
from jax import lax
from jax.experimental import pallas as pl
from jax.experimental.pallas import tpu as pltpu

EPS = 1e-6
CONV_K = 3
CHUNK = 128
HEAD_DIM = 128
N_SECT = 7
X_HEADS = 4
SUBLANES = 8

TM = 1024
TM_ATTN = 1024
ATTN_HALF = 512
CW = 256
MIX_STAGES = 3
ATTN_STAGES = 3
STAGE_BYTES = 1 << 21
MIX_VMEM_LIMIT = 63 * 1024 * 1024
ATTN_VMEM_LIMIT = 56 * 1024 * 1024
PARAM_ROWS = 4


def _rms(x, g):
    ms = jnp.mean(x * x, axis=-1, keepdims=True)
    return x * lax.rsqrt(ms + EPS) * g


def _sigmoid(z):
    return 0.5 * (1.0 + jnp.tanh(0.5 * z))


def _gelu_tanh(x):
    c = math.sqrt(2.0 / math.pi)
    return 0.5 * x * (1.0 + jnp.tanh(c * (x + 0.044715 * (x * x * x))))


def _load_weights_bf16(jobs, sem_ref):
    n_stage = sem_ref.shape[0]
    blocks = [(src, dst, stage, r, scale)
              for src, dst, stage, scale in jobs
              for r in range(0, src.shape[0], stage.shape[1])]

    def copy(k):
        src, _, stage, r, _ = blocks[k]
        return pltpu.make_async_copy(src.at[pl.ds(r, stage.shape[1])],
                                     stage.at[k % n_stage], sem_ref.at[k % n_stage])

    for k in range(min(n_stage - 1, len(blocks))):
        copy(k).start()
    for k, (_, dst, stage, r, scale) in enumerate(blocks):
        if k + n_stage - 1 < len(blocks):
            copy(k + n_stage - 1).start()
        copy(k).wait()
        slab = stage[k % n_stage]
        if scale is not None:
            slab = slab * scale[r:r + stage.shape[1], :]
        dst[r:r + stage.shape[1], :] = slab.astype(jnp.bfloat16)


def _first_grid_step():
    return (pl.program_id(0) == 0) & (pl.program_id(1) == 0)


def _mixer_kernel(x_ref, g_ref, w_in_hbm, conv_w_hbm, ln_g_ref, ln_b_ref,
                  ws_ref, bs_ref, w_out_hbm, o_ref,
                  w_in_ref, w_out_ref, stage_in_ref, stage_out_ref, sem_ref,
                  h_ref, proj_ref, carry_ref, mix_ref, bst_ref, conv_w_ref, conv_sem):
    tm, d = x_ref.shape
    n_col_chunks = d // CW
    n_row_blocks = tm // CHUNK
    heads_per_chunk = CW // HEAD_DIM

    @pl.when(_first_grid_step())
    def _():
        conv_copy = pltpu.make_async_copy(conv_w_hbm, conv_w_ref.at[pl.ds(0, CONV_K)], conv_sem)
        conv_copy.start()
        _load_weights_bf16([(w_in_hbm, w_in_ref, stage_in_ref, None),
                            (w_out_hbm, w_out_ref, stage_out_ref, None)], sem_ref)
        conv_copy.wait()

    @pl.when(pl.program_id(1) == 0)
    def _():
        carry_ref[...] = jnp.zeros_like(carry_ref)

    h_ref[...] = _rms(x_ref[...], g_ref[...]).astype(jnp.bfloat16)
    bst_ref[...] = bs_ref[...].T

    tri = (lax.broadcasted_iota(jnp.int32, (CHUNK, CHUNK), 0)
           >= lax.broadcasted_iota(jnp.int32, (CHUNK, CHUNK), 1))

    def in_proj(j):
        for s in (5, 1, 2, 3, 0, 4, 6):
            proj_ref[j % 2, :, s * CW:(s + 1) * CW] = jnp.dot(
                h_ref[...], w_in_ref[:, s * d + j * CW:s * d + (j + 1) * CW],
                preferred_element_type=jnp.float32)

    def out_proj(c0, c1):
        part = [jnp.dot(mix_ref[:, o + c0:o + c1], w_out_ref[o + c0:o + c1, :],
                        preferred_element_type=jnp.float32) for o in (0, d)]
        return part[0] + part[1]

    in_proj(0)
    for j in range(n_col_chunks):
        cols = slice(j * CW, (j + 1) * CW)
        if j + 1 < n_col_chunks:
            in_proj(j + 1)
        else:
            o_ref[...] = x_ref[...] + out_proj(0, j * CW)

        def sect(s, rows, j=j):
            return proj_ref[j % 2, rows, s * CW:(s + 1) * CW]

        w0 = conv_w_ref[0, :, cols]
        w1 = conv_w_ref[1, :, cols]
        w2 = conv_w_ref[2, :, cols]
        wc = [jnp.where(tri, ws_ref[j * heads_per_chunk + hh], 0.0).astype(jnp.bfloat16)
              for hh in range(heads_per_chunk)]

        for r in range(n_row_blocks):
            rows = slice(r * CHUNK, (r + 1) * CHUNK)
            g = sect(1, rows) * sect(2, rows)
            if r == 0:
                prev = carry_ref[:, cols]
            else:
                prows = slice(r * CHUNK - SUBLANES, r * CHUNK)
                prev = sect(1, prows) * sect(2, prows)
            if r == n_row_blocks - 1:
                carry_ref[:, cols] = g[CHUNK - SUBLANES:, :]
            ext = jnp.concatenate([prev, g], axis=0)
            g1 = ext[SUBLANES - 1:SUBLANES - 1 + CHUNK, :]
            g2 = ext[SUBLANES - 2:SUBLANES - 2 + CHUNK, :]
            conv = w0 * g2 + w1 * g1 + w2 * g
            za = sect(3, rows)
            a = sect(0, rows) * conv * (za * _sigmoid(za))
            mix_ref[rows, cols] = a.astype(jnp.bfloat16)

            u = _gelu_tanh(sect(4, rows))
            v = _gelu_tanh(sect(5, rows))
            zb = sect(6, rows)
            sps = []
            for hh in range(heads_per_chunk):
                hc = slice(hh * HEAD_DIM, (hh + 1) * HEAD_DIM)
                gcols = slice(j * CW + hh * HEAD_DIM, j * CW + (hh + 1) * HEAD_DIM)
                vh = v[:, hc]
                mu = jnp.mean(vh, axis=-1, keepdims=True)
                vc = vh - mu
                var = jnp.mean(vc * vc, axis=-1, keepdims=True)
                vn = vc * lax.rsqrt(var + EPS) * ln_g_ref[:, gcols] + ln_b_ref[:, gcols]
                sp = jnp.dot(wc[hh], vn.astype(jnp.bfloat16),
                             preferred_element_type=jnp.float32)
                head = j * heads_per_chunk + hh
                sps.append(sp + bst_ref[:, head:head + 1])
            sp = jnp.concatenate(sps, axis=-1)
            bo = u * sp * (zb * _sigmoid(zb))
            mix_ref[rows, d + j * CW:d + (j + 1) * CW] = bo.astype(jnp.bfloat16)

    o_ref[...] += out_proj((n_col_chunks - 1) * CW, d)


def _attn_kernel(x_ref, mem_ref, gx_ref, gm_ref, w_q_hbm, w_kv_hbm, w_xo_hbm, gf_ref,
                 o_ref,
                 w_q_ref, w_kv_ref, w_xo_ref, stage_d_ref, stage_kv_ref, sem_ref,
                 wqk_ref, vxo_ref, p_ref):
    tm, d = x_ref.shape
    m_len = mem_ref.shape[0]
    hd = d // X_HEADS
    bf16 = jnp.bfloat16

    @pl.when(_first_grid_step())
    def _():
        g_col = jnp.broadcast_to(gx_ref[...], (SUBLANES, d)).T[:, 0:1]
        _load_weights_bf16([(w_q_hbm, w_q_ref, stage_d_ref, g_col),
                            (w_kv_hbm, w_kv_ref, stage_kv_ref, None),
                            (w_xo_hbm, w_xo_ref, stage_d_ref, None)], sem_ref)

    @pl.when(pl.program_id(1) == 0)
    def _():
        m = _rms(mem_ref[...], gm_ref[...]).astype(bf16)
        kv = jnp.dot(m, w_kv_ref[...], preferred_element_type=jnp.float32)
        for hh in range(X_HEADS):
            hc = slice(hh * hd, (hh + 1) * hd)
            mc = slice(hh * m_len, (hh + 1) * m_len)
            k_h = kv[:, hc].astype(bf16)
            v_h = kv[:, d + hh * hd:d + (hh + 1) * hd].astype(bf16)
            wqk = lax.dot_general(w_q_ref[:, hc], k_h, (((1,), (1,)), ((), ())),
                                  preferred_element_type=jnp.float32)
            wqk_ref[:, mc] = (wqk * (1.0 / math.sqrt(hd))).astype(bf16)
            vxo_ref[mc, :] = jnp.dot(v_h, w_xo_ref[hc, :],
                                     preferred_element_type=jnp.float32).astype(bf16)

    halves = [slice(i * ATTN_HALF, (i + 1) * ATTN_HALF) for i in range(tm // ATTN_HALF)]
    xs = [x_ref[r, :] for r in halves]
    scores = [jnp.dot(x.astype(bf16), wqk_ref[...], preferred_element_type=jnp.float32)
              for x in xs]
    rinv = [lax.rsqrt(jnp.mean(x * x, axis=-1, keepdims=True) + EPS) for x in xs]
    for hh in range(X_HEADS):
        mc = slice(hh * m_len, (hh + 1) * m_len)
        for s, ri, r in zip(scores, rinv, halves):
            z = s[:, mc] - jnp.max(s[:, mc], axis=-1, keepdims=True)
            p = jnp.exp(jnp.where(ri > 0.0, ri * z, 0.0))
            l = jnp.sum(p, axis=-1, keepdims=True)
            p_ref[r, mc] = (p * (1.0 / l)).astype(bf16)
    for r in halves:
        y = x_ref[r, :] + jnp.dot(p_ref[r, :], vxo_ref[...],
                                  preferred_element_type=jnp.float32)
        o_ref[r, :] = _rms(y, gf_ref[...])


def _stage(cols, n_stage):
    rows = 1 << int(math.log2(STAGE_BYTES // (4 * cols)))
    return pltpu.VMEM((n_stage, rows, cols), jnp.float32)


def _resident(shape):
    return pl.BlockSpec(shape, lambda *_: (0,) * len(shape),
                        pipeline_mode=pl.Buffered(1))


def _resident_layer(shape):
    return pl.BlockSpec((None,) + tuple(shape[1:]), lambda *_: (0,) * len(shape),
                        pipeline_mode=pl.Buffered(1))


def kernel(x, mem, norm_mix_g, w_in, conv_w, gm_ln_g, gm_ln_b, gm_ws, gm_bs, w_out,
           norm_x_g, norm_mem_g, w_q, w_kv, w_xo, norm_final_g):
    b, s, d = x.shape
    m_len = mem.shape[1]
    assert w_in.shape[0] == 1, "the final norm is fused into the (single) layer's attention call"
    assert s % TM == 0 and TM % CHUNK == 0 and d % CW == 0
    assert s % TM_ATTN == 0 and TM_ATTN % ATTN_HALF == 0
    bf16 = jnp.bfloat16
    f32 = jnp.float32
    params = lambda limit: pltpu.CompilerParams(
        dimension_semantics=("arbitrary", "arbitrary"), vmem_limit_bytes=limit)
    row = lambda a: a.reshape(1, -1)
    hbm = pl.BlockSpec(memory_space=pl.ANY)
    tile = pl.BlockSpec((None, TM, d), lambda i, t: (i, t, 0))
    sems = lambda n: pltpu.SemaphoreType.DMA((n,))

    x = pl.pallas_call(
        _mixer_kernel,
        grid=(b, s // TM),
        in_specs=[tile,
                  _resident((1, d)),
                  hbm,
                  hbm,
                  _resident((1, d)), _resident((1, d)),
                  _resident(gm_ws.shape[1:]),
                  _resident_layer(gm_bs.shape),
                  hbm],
        out_specs=tile,
        out_shape=jax.ShapeDtypeStruct((b, s, d), f32),
        scratch_shapes=[pltpu.VMEM((d, N_SECT * d), bf16),
                        pltpu.VMEM((2 * d, d), bf16),
                        _stage(N_SECT * d, MIX_STAGES), _stage(d, MIX_STAGES),
                        sems(MIX_STAGES),
                        pltpu.VMEM((TM, d), bf16),
                        pltpu.VMEM((2, TM, N_SECT * CW), f32),
                        pltpu.VMEM((SUBLANES, d), f32),
                        pltpu.VMEM((TM, 2 * d), bf16),
                        pltpu.VMEM((CHUNK, gm_bs.shape[1]), f32),
                        pltpu.VMEM((PARAM_ROWS, 1, d), f32),
                        pltpu.SemaphoreType.DMA(())],
        compiler_params=params(MIX_VMEM_LIMIT),
        name="mixer",
    )(x, row(norm_mix_g[0]), w_in[0], conv_w.reshape(CONV_K, 1, d), row(gm_ln_g[0]), row(gm_ln_b[0]),
      gm_ws[0], gm_bs, w_out[0])

    attn_tile = pl.BlockSpec((None, TM_ATTN, d), lambda i, t: (i, t, 0))
    return pl.pallas_call(
        _attn_kernel,
        grid=(b, s // TM_ATTN),
        in_specs=[attn_tile,
                  pl.BlockSpec((None, m_len, d), lambda i, t: (i, 0, 0)),
                  _resident((1, d)), _resident((1, d)),
                  hbm, hbm, hbm,
                  _resident((1, d))],
        out_specs=attn_tile,
        out_shape=jax.ShapeDtypeStruct((b, s, d), f32),
        scratch_shapes=[pltpu.VMEM((d, d), bf16),
                        pltpu.VMEM((d, 2 * d), bf16),
                        pltpu.VMEM((d, d), bf16),
                        _stage(d, ATTN_STAGES), _stage(2 * d, ATTN_STAGES),
                        sems(ATTN_STAGES),
                        pltpu.VMEM((d, X_HEADS * m_len), bf16),
                        pltpu.VMEM((X_HEADS * m_len, d), bf16),
                        pltpu.VMEM((TM_ATTN, X_HEADS * m_len), bf16)],
        compiler_params=params(ATTN_VMEM_LIMIT),
        name="xattn",
    )(x, mem, row(norm_x_g[0]), row(norm_mem_g[0]), w_q[0], w_kv[0], w_xo[0],
      row(norm_final_g))
```

```python
import math

import jax
import jax.numpy as jnp
from jax import lax
from jax.experimental import pallas as pl
from jax.experimental.pallas import tpu as pltpu

EPS = 1e-6
CONV_K = 3
CHUNK = 128
HEAD_DIM = 128
N_SECT = 7
X_HEADS = 4
SUBLANES = 8

TM = 1024
TM_ATTN = 1024
ATTN_HALF = 512
CW = 256
MIX_STAGES = 3
ATTN_STAGES = 3
STAGE_BYTES = 1 << 21
MIX_VMEM_LIMIT = 63 * 1024 * 1024
ATTN_VMEM_LIMIT = 56 * 1024 * 1024
PARAM_ROWS = 4


def _rms(x, g):
    ms = jnp.mean(x * x, axis=-1, keepdims=True)
    return x * lax.rsqrt(ms + EPS) * g


def _sigmoid(z):
    return 0.5 * (1.0 + jnp.tanh(0.5 * z))


def _gelu_tanh(x):
    c = math.sqrt(2.0 / math.pi)
    return 0.5 * x * (1.0 + jnp.tanh(c * (x + 0.044715 * (x * x * x))))


def _load_weights_bf16(jobs, sem_ref):
    n_stage = sem_ref.shape[0]
    blocks = [(src, dst, stage, r, scale)
              for src, dst, stage, scale in jobs
              for r in range(0, src.shape[0], stage.shape[1])]

    def copy(k):
        src, _, stage, r, _ = blocks[k]
        return pltpu.make_async_copy(src.at[pl.ds(r, stage.shape[1])],
                                     stage.at[k % n_stage], sem_ref.at[k % n_stage])

    for k in range(min(n_stage - 1, len(blocks))):
        copy(k).start()
    for k, (_, dst, stage, r, scale) in enumerate(blocks):
        if k + n_stage - 1 < len(blocks):
            copy(k + n_stage - 1).start()
        copy(k).wait()
        slab = stage[k % n_stage]
        if scale is not None:
            slab = slab * scale[r:r + stage.shape[1], :]
        dst[r:r + stage.shape[1], :] = slab.astype(jnp.bfloat16)


def _first_grid_step():
    return (pl.program_id(0) == 0) & (pl.program_id(1) == 0)


def _mixer_kernel(x_ref, g_ref, w_in_hbm, conv_w_hbm, ln_g_ref, ln_b_ref,
                  ws_ref, bs_ref, w_out_hbm, o_ref,
                  w_in_ref, w_out_ref, stage_in_ref, stage_out_ref, sem_ref,
                  h_ref, proj_ref, carry_ref, mix_ref, bst_ref, conv_w_ref, conv_sem):
    tm, d = x_ref.shape
    n_col_chunks = d // CW
    n_row_blocks = tm // CHUNK
    heads_per_chunk = CW // HEAD_DIM

    @pl.when(_first_grid_step())
    def _():
        conv_copy = pltpu.make_async_copy(conv_w_hbm, conv_w_ref.at[pl.ds(0, CONV_K)], conv_sem)
        conv_copy.start()
        _load_weights_bf16([(w_in_hbm, w_in_ref, stage_in_ref, None),
                            (w_out_hbm, w_out_ref, stage_out_ref, None)], sem_ref)
        conv_copy.wait()

    @pl.when(pl.program_id(1) == 0)
    def _():
        carry_ref[...] = jnp.zeros_like(carry_ref)

    h_ref[...] = _rms(x_ref[...], g_ref[...]).astype(jnp.bfloat16)
    bst_ref[...] = bs_ref[...].T

    tri = (lax.broadcasted_iota(jnp.int32, (CHUNK, CHUNK), 0)
           >= lax.broadcasted_iota(jnp.int32, (CHUNK, CHUNK), 1))

    def in_proj(j):
        for s in (5, 1, 2, 3, 0, 4, 6):
            proj_ref[j % 2, :, s * CW:(s + 1) * CW] = jnp.dot(
                h_ref[...], w_in_ref[:, s * d + j * CW:s * d + (j + 1) * CW],
                preferred_element_type=jnp.float32)

    def out_proj(c0, c1):
        part = [jnp.dot(mix_ref[:, o + c0:o + c1], w_out_ref[o + c0:o + c1, :],
                        preferred_element_type=jnp.float32) for o in (0, d)]
        return part[0] + part[1]

    in_proj(0)
    for j in range(n_col_chunks):
        cols = slice(j * CW, (j + 1) * CW)
        if j + 1 < n_col_chunks:
            in_proj(j + 1)
        else:
            o_ref[...] = x_ref[...] + out_proj(0, j * CW)

        def sect(s, rows, j=j):
            return proj_ref[j % 2, rows, s * CW:(s + 1) * CW]

        w0 = conv_w_ref[0, :, cols]
        w1 = conv_w_ref[1, :, cols]
        w2 = conv_w_ref[2, :, cols]
        wc = [jnp.where(tri, ws_ref[j * heads_per_chunk + hh], 0.0).astype(jnp.bfloat16)
              for hh in range(heads_per_chunk)]

        for r in range(n_row_blocks):
            rows = slice(r * CHUNK, (r + 1) * CHUNK)
            g = sect(1, rows) * sect(2, rows)
            if r == 0:
                prev = carry_ref[:, cols]
            else:
                prows = slice(r * CHUNK - SUBLANES, r * CHUNK)
                prev = sect(1, prows) * sect(2, prows)
            if r == n_row_blocks - 1:
                carry_ref[:, cols] = g[CHUNK - SUBLANES:, :]
            ext = jnp.concatenate([prev, g], axis=0)
            g1 = ext[SUBLANES - 1:SUBLANES - 1 + CHUNK, :]
            g2 = ext[SUBLANES - 2:SUBLANES - 2 + CHUNK, :]
            conv = w0 * g2 + w1 * g1 + w2 * g
            za = sect(3, rows)
            a = sect(0, rows) * conv * (za * _sigmoid(za))
            mix_ref[rows, cols] = a.astype(jnp.bfloat16)

            u = _gelu_tanh(sect(4, rows))
            v = _gelu_tanh(sect(5, rows))
            zb = sect(6, rows)
            sps = []
            for hh in range(heads_per_chunk):
                hc = slice(hh * HEAD_DIM, (hh + 1) * HEAD_DIM)
                gcols = slice(j * CW + hh * HEAD_DIM, j * CW + (hh + 1) * HEAD_DIM)
                vh = v[:, hc]
                mu = jnp.mean(vh, axis=-1, keepdims=True)
                vc = vh - mu
                var = jnp.mean(vc * vc, axis=-1, keepdims=True)
                vn = vc * lax.rsqrt(var + EPS) * ln_g_ref[:, gcols] + ln_b_ref[:, gcols]
                sp = jnp.dot(wc[hh], vn.astype(jnp.bfloat16),
                             preferred_element_type=jnp.float32)
                head = j * heads_per_chunk + hh
                sps.append(sp + bst_ref[:, head:head + 1])
            sp = jnp.concatenate(sps, axis=-1)
            bo = u * sp * (zb * _sigmoid(zb))
            mix_ref[rows, d + j * CW:d + (j + 1) * CW] = bo.astype(jnp.bfloat16)

    o_ref[...] += out_proj((n_col_chunks - 1) * CW, d)


def _attn_kernel(x_ref, mem_ref, gx_ref, gm_ref, w_q_hbm, w_kv_hbm, w_xo_hbm, gf_ref,
                 o_ref,
                 w_q_ref, w_kv_ref, w_xo_ref, stage_d_ref, stage_kv_ref, sem_ref,
                 wqk_ref, vxo_ref, p_ref):
    tm, d = x_ref.shape
    m_len = mem_ref.shape[0]
    hd = d // X_HEADS
    bf16 = jnp.bfloat16

    @pl.when(_first_grid_step())
    def _():
        g_col = jnp.broadcast_to(gx_ref[...], (SUBLANES, d)).T[:, 0:1]
        _load_weights_bf16([(w_q_hbm, w_q_ref, stage_d_ref, g_col),
                            (w_kv_hbm, w_kv_ref, stage_kv_ref, None),
                            (w_xo_hbm, w_xo_ref, stage_d_ref, None)], sem_ref)

    @pl.when(pl.program_id(1) == 0)
    def _():
        m = _rms(mem_ref[...], gm_ref[...]).astype(bf16)
        kv = jnp.dot(m, w_kv_ref[...], preferred_element_type=jnp.float32)
        for hh in range(X_HEADS):
            hc = slice(hh * hd, (hh + 1) * hd)
            mc = slice(hh * m_len, (hh + 1) * m_len)
            k_h = kv[:, hc].astype(bf16)
            v_h = kv[:, d + hh * hd:d + (hh + 1) * hd].astype(bf16)
            wqk = lax.dot_general(w_q_ref[:, hc], k_h, (((1,), (1,)), ((), ())),
                                  preferred_element_type=jnp.float32)
            wqk_ref[:, mc] = (wqk * (1.0 / math.sqrt(hd))).astype(bf16)
            vxo_ref[mc, :] = jnp.dot(v_h, w_xo_ref[hc, :],
                                     preferred_element_type=jnp.float32).astype(bf16)

    halves = [slice(i * ATTN_HALF, (i + 1) * ATTN_HALF) for i in range(tm // ATTN_HALF)]
    xs = [x_ref[r, :] for r in halves]
    scores = [jnp.dot(x.astype(bf16), wqk_ref[...], preferred_element_type=jnp.float32)
              for x in xs]
    rinv = [lax.rsqrt(jnp.mean(x * x, axis=-1, keepdims=True) + EPS) for x in xs]
    for hh in range(X_HEADS):
        mc = slice(hh * m_len, (hh + 1) * m_len)
        for s, ri, r in zip(scores, rinv, halves):
            z = s[:, mc] - jnp.max(s[:, mc], axis=-1, keepdims=True)
            p = jnp.exp(jnp.where(ri > 0.0, ri * z, 0.0))
            l = jnp.sum(p, axis=-1, keepdims=True)
            p_ref[r, mc] = (p * (1.0 / l)).astype(bf16)
    for r in halves:
        o = None
        for hh in range(X_HEADS):
            mc = slice(hh * m_len, (hh + 1) * m_len)
            part = jnp.dot(p_ref[r, mc], vxo_ref[mc, :], preferred_element_type=jnp.float32)
            o = part if o is None else o + part
        y = x_ref[r, :] + o
        o_ref[r, :] = _rms(y, gf_ref[...])


def _stage(cols, n_stage):
    rows = 1 << int(math.log2(STAGE_BYTES // (4 * cols)))
    return pltpu.VMEM((n_stage, rows, cols), jnp.float32)


def _resident(shape):
    return pl.BlockSpec(shape, lambda *_: (0,) * len(shape),
                        pipeline_mode=pl.Buffered(1))


def _resident_layer(shape):
    return pl.BlockSpec((None,) + tuple(shape[1:]), lambda *_: (0,) * len(shape),
                        pipeline_mode=pl.Buffered(1))


def kernel(x, mem, norm_mix_g, w_in, conv_w, gm_ln_g, gm_ln_b, gm_ws, gm_bs, w_out,
           norm_x_g, norm_mem_g, w_q, w_kv, w_xo, norm_final_g):
    b, s, d = x.shape
    m_len = mem.shape[1]
    assert w_in.shape[0] == 1, "the final norm is fused into the (single) layer's attention call"
    assert s % TM == 0 and TM % CHUNK == 0 and d % CW == 0
    assert s % TM_ATTN == 0 and TM_ATTN % ATTN_HALF == 0
    bf16 = jnp.bfloat16
    f32 = jnp.float32
    params = lambda limit: pltpu.CompilerParams(
        dimension_semantics=("arbitrary", "arbitrary"), vmem_limit_bytes=limit)
    row = lambda a: a.reshape(1, -1)
    hbm = pl.BlockSpec(memory_space=pl.ANY)
    tile = pl.BlockSpec((None, TM, d), lambda i, t: (i, t, 0))
    sems = lambda n: pltpu.SemaphoreType.DMA((n,))

    x = pl.pallas_call(
        _mixer_kernel,
        grid=(b, s // TM),
        in_specs=[tile,
                  _resident((1, d)),
                  hbm,
                  hbm,
                  _resident((1, d)), _resident((1, d)),
                  _resident(gm_ws.shape[1:]),
                  _resident_layer(gm_bs.shape),
                  hbm],
        out_specs=tile,
        out_shape=jax.ShapeDtypeStruct((b, s, d), f32),
        scratch_shapes=[pltpu.VMEM((d, N_SECT * d), bf16),
                        pltpu.VMEM((2 * d, d), bf16),
                        _stage(N_SECT * d, MIX_STAGES), _stage(d, MIX_STAGES),
                        sems(MIX_STAGES),
                        pltpu.VMEM((TM, d), bf16),
                        pltpu.VMEM((2, TM, N_SECT * CW), f32),
                        pltpu.VMEM((SUBLANES, d), f32),
                        pltpu.VMEM((TM, 2 * d), bf16),
                        pltpu.VMEM((CHUNK, gm_bs.shape[1]), f32),
                        pltpu.VMEM((PARAM_ROWS, 1, d), f32),
                        pltpu.SemaphoreType.DMA(())],
        compiler_params=params(MIX_VMEM_LIMIT),
        name="mixer",
    )(x, row(norm_mix_g[0]), w_in[0], conv_w.reshape(CONV_K, 1, d), row(gm_ln_g[0]), row(gm_ln_b[0]),
      gm_ws[0], gm_bs, w_out[0])

    attn_tile = pl.BlockSpec((None, TM_ATTN, d), lambda i, t: (i, t, 0))
    return pl.pallas_call(
        _attn_kernel,
        grid=(b, s // TM_ATTN),
        in_specs=[attn_tile,
                  pl.BlockSpec((None, m_len, d), lambda i, t: (i, 0, 0)),
                  _resident((1, d)), _resident((1, d)),
                  hbm, hbm, hbm,
                  _resident((1, d))],
        out_specs=attn_tile,
        out_shape=jax.ShapeDtypeStruct((b, s, d), f32),
        scratch_shapes=[pltpu.VMEM((d, d), bf16),
                        pltpu.VMEM((d, 2 * d), bf16),
                        pltpu.VMEM((d, d), bf16),
                        _stage(d, ATTN_STAGES), _stage(2 * d, ATTN_STAGES),
                        sems(ATTN_STAGES),
                        pltpu.VMEM((d, X_HEADS * m_len), bf16),
                        pltpu.VMEM((X_HEADS * m_len, d), bf16),
                        pltpu.VMEM((TM_ATTN, X_HEADS * m_len), bf16)],
        compiler_params=params(ATTN_VMEM_LIMIT),
        name="xattn",
    )(x, mem, row(norm_x_g[0]), row(norm_mem_g[0]), w_q[0], w_kv[0], w_xo[0],
      row(norm_final_g))
```

```python
import math

import jax
import jax.numpy as jnp
from jax import lax
from jax.experimental import pallas as pl
from jax.experimental.pallas import tpu as pltpu

EPS = 1e-6
CONV_K = 3
CHUNK = 128
HEAD_DIM = 128
N_SECT = 7
X_HEADS = 4
SUBLANES = 8

TM = 1024
TM_ATTN = 1024
ATTN_HALF = 512
CW = 256
MIX_STAGES = 3
ATTN_STAGES = 3
STAGE_BYTES = 1 << 21
MIX_VMEM_LIMIT = 63 * 1024 * 1024
ATTN_VMEM_LIMIT = 56 * 1024 * 1024
PARAM_ROWS = 4


def _rms(x, g):
    ms = jnp.mean(x * x, axis=-1, keepdims=True)
    return x * lax.rsqrt(ms + EPS) * g


def _sigmoid(z):
    return 0.5 * (1.0 + jnp.tanh(0.5 * z))


def _gelu_tanh(x):
    c = math.sqrt(2.0 / math.pi)
    return 0.5 * x * (1.0 + jnp.tanh(c * (x + 0.044715 * (x * x * x))))


def _load_weights_bf16(jobs, sem_ref):
    n_stage = sem_ref.shape[0]
    blocks = [(src, dst, stage, r, scale)
              for src, dst, stage, scale in jobs
              for r in range(0, src.shape[0], stage.shape[1])]

    def copy(k):
        src, _, stage, r, _ = blocks[k]
        return pltpu.make_async_copy(src.at[pl.ds(r, stage.shape[1])],
                                     stage.at[k % n_stage], sem_ref.at[k % n_stage])

    for k in range(min(n_stage - 1, len(blocks))):
        copy(k).start()
    for k, (_, dst, stage, r, scale) in enumerate(blocks):
        if k + n_stage - 1 < len(blocks):
            copy(k + n_stage - 1).start()
        copy(k).wait()
        slab = stage[k % n_stage]
        if scale is not None:
            slab = slab * scale[r:r + stage.shape[1], :]
        dst[r:r + stage.shape[1], :] = slab.astype(jnp.bfloat16)


def _first_grid_step():
    return (pl.program_id(0) == 0) & (pl.program_id(1) == 0)


def _mixer_kernel(x_ref, g_ref, w_in_hbm, conv_w_hbm, ln_g_ref, ln_b_ref,
                  ws_ref, bs_ref, w_out_hbm, o_ref,
                  w_in_ref, w_out_ref, stage_in_ref, stage_out_ref, sem_ref,
                  h_ref, proj_ref, carry_ref, mix_ref, bst_ref, conv_w_ref, conv_sem):
    tm, d = x_ref.shape
    n_col_chunks = d // CW
    n_row_blocks = tm // CHUNK
    heads_per_chunk = CW // HEAD_DIM

    @pl.when(_first_grid_step())
    def _():
        conv_copy = pltpu.make_async_copy(conv_w_hbm, conv_w_ref.at[pl.ds(0, CONV_K)], conv_sem)
        conv_copy.start()
        _load_weights_bf16([(w_in_hbm, w_in_ref, stage_in_ref, None),
                            (w_out_hbm, w_out_ref, stage_out_ref, None)], sem_ref)
        conv_copy.wait()

    @pl.when(pl.program_id(1) == 0)
    def _():
        carry_ref[...] = jnp.zeros_like(carry_ref)

    h_ref[...] = _rms(x_ref[...], g_ref[...]).astype(jnp.bfloat16)
    bst_ref[...] = bs_ref[...].T

    tri = (lax.broadcasted_iota(jnp.int32, (CHUNK, CHUNK), 0)
           >= lax.broadcasted_iota(jnp.int32, (CHUNK, CHUNK), 1))

    def in_proj(j):
        for s in (5, 1, 2, 0, 3, 4, 6):
            proj_ref[j % 2, :, s * CW:(s + 1) * CW] = jnp.dot(
                h_ref[...], w_in_ref[:, s * d + j * CW:s * d + (j + 1) * CW],
                preferred_element_type=jnp.float32)

    def out_proj(c0, c1):
        part = [jnp.dot(mix_ref[:, o + c0:o + c1], w_out_ref[o + c0:o + c1, :],
                        preferred_element_type=jnp.float32) for o in (0, d)]
        return part[0] + part[1]

    in_proj(0)
    for j in range(n_col_chunks):
        cols = slice(j * CW, (j + 1) * CW)
        if j + 1 < n_col_chunks:
            in_proj(j + 1)
        else:
            o_ref[...] = x_ref[...] + out_proj(0, j * CW)

        def sect(s, rows, j=j):
            return proj_ref[j % 2, rows, s * CW:(s + 1) * CW]

        w0 = conv_w_ref[0, :, cols]
        w1 = conv_w_ref[1, :, cols]
        w2 = conv_w_ref[2, :, cols]
        wc = [jnp.where(tri, ws_ref[j * heads_per_chunk + hh], 0.0).astype(jnp.bfloat16)
              for hh in range(heads_per_chunk)]

        for r in range(n_row_blocks):
            rows = slice(r * CHUNK, (r + 1) * CHUNK)
            g = sect(1, rows) * sect(2, rows)
            if r == 0:
                prev = carry_ref[:, cols]
            else:
                prows = slice(r * CHUNK - SUBLANES, r * CHUNK)
                prev = sect(1, prows) * sect(2, prows)
            if r == n_row_blocks - 1:
                carry_ref[:, cols] = g[CHUNK - SUBLANES:, :]
            ext = jnp.concatenate([prev, g], axis=0)
            g1 = ext[SUBLANES - 1:SUBLANES - 1 + CHUNK, :]
            g2 = ext[SUBLANES - 2:SUBLANES - 2 + CHUNK, :]
            conv = w0 * g2 + w1 * g1 + w2 * g
            za = sect(3, rows)
            a = sect(0, rows) * conv * (za * _sigmoid(za))
            mix_ref[rows, cols] = a.astype(jnp.bfloat16)

            u = _gelu_tanh(sect(4, rows))
            v = _gelu_tanh(sect(5, rows))
            zb = sect(6, rows)
            sps = []
            for hh in range(heads_per_chunk):
                hc = slice(hh * HEAD_DIM, (hh + 1) * HEAD_DIM)
                gcols = slice(j * CW + hh * HEAD_DIM, j * CW + (hh + 1) * HEAD_DIM)
                vh = v[:, hc]
                mu = jnp.mean(vh, axis=-1, keepdims=True)
                vc = vh - mu
                var = jnp.mean(vc * vc, axis=-1, keepdims=True)
                vn = vc * lax.rsqrt(var + EPS) * ln_g_ref[:, gcols] + ln_b_ref[:, gcols]
                sp = jnp.dot(wc[hh], vn.astype(jnp.bfloat16),
                             preferred_element_type=jnp.float32)
                head = j * heads_per_chunk + hh
                sps.append(sp + bst_ref[:, head:head + 1])
            sp = jnp.concatenate(sps, axis=-1)
            bo = u * sp * (zb * _sigmoid(zb))
            mix_ref[rows, d + j * CW:d + (j + 1) * CW] = bo.astype(jnp.bfloat16)

    o_ref[...] += out_proj((n_col_chunks - 1) * CW, d)


def _attn_kernel(x_ref, mem_ref, gx_ref, gm_ref, w_q_hbm, w_kv_hbm, w_xo_hbm, gf_ref,
                 o_ref,
                 w_q_ref, w_kv_ref, w_xo_ref, stage_d_ref, stage_kv_ref, sem_ref,
                 wqk_ref, vxo_ref, p_ref):
    tm, d = x_ref.shape
    m_len = mem_ref.shape[0]
    hd = d // X_HEADS
    bf16 = jnp.bfloat16

    @pl.when(_first_grid_step())
    def _():
        g_col = jnp.broadcast_to(gx_ref[...], (SUBLANES, d)).T[:, 0:1]
        _load_weights_bf16([(w_q_hbm, w_q_ref, stage_d_ref, g_col),
                            (w_kv_hbm, w_kv_ref, stage_kv_ref, None),
                            (w_xo_hbm, w_xo_ref, stage_d_ref, None)], sem_ref)

    @pl.when(pl.program_id(1) == 0)
    def _():
        m = _rms(mem_ref[...], gm_ref[...]).astype(bf16)
        kv = jnp.dot(m, w_kv_ref[...], preferred_element_type=jnp.float32)
        for hh in range(X_HEADS):
            hc = slice(hh * hd, (hh + 1) * hd)
            mc = slice(hh * m_len, (hh + 1) * m_len)
            k_h = kv[:, hc].astype(bf16)
            v_h = kv[:, d + hh * hd:d + (hh + 1) * hd].astype(bf16)
            wqk = lax.dot_general(w_q_ref[:, hc], k_h, (((1,), (1,)), ((), ())),
                                  preferred_element_type=jnp.float32)
            wqk_ref[:, mc] = (wqk * (1.0 / math.sqrt(hd))).astype(bf16)
            vxo_ref[mc, :] = jnp.dot(v_h, w_xo_ref[hc, :],
                                     preferred_element_type=jnp.float32).astype(bf16)

    halves = [slice(i * ATTN_HALF, (i + 1) * ATTN_HALF) for i in range(tm // ATTN_HALF)]
    xs = [x_ref[r, :] for r in halves]
    scores = [jnp.dot(x.astype(bf16), wqk_ref[...], preferred_element_type=jnp.float32)
              for x in xs]
    rinv = [lax.rsqrt(jnp.mean(x * x, axis=-1, keepdims=True) + EPS) for x in xs]
    for hh in range(X_HEADS):
        mc = slice(hh * m_len, (hh + 1) * m_len)
        for s, ri, r in zip(scores, rinv, halves):
            z = s[:, mc] - jnp.max(s[:, mc], axis=-1, keepdims=True)
            p = jnp.exp(jnp.where(ri > 0.0, ri * z, 0.0))
            l = jnp.sum(p, axis=-1, keepdims=True)
            p_ref[r, mc] = (p * (1.0 / l)).astype(bf16)
    for r in halves:
        y = x_ref[r, :] + jnp.dot(p_ref[r, :], vxo_ref[...],
                                  preferred_element_type=jnp.float32)
        o_ref[r, :] = _rms(y, gf_ref[...])


def _stage(cols, n_stage):
    rows = 1 << int(math.log2(STAGE_BYTES // (4 * cols)))
    return pltpu.VMEM((n_stage, rows, cols), jnp.float32)


def _resident(shape):
    return pl.BlockSpec(shape, lambda *_: (0,) * len(shape),
                        pipeline_mode=pl.Buffered(1))


def _resident_layer(shape):
    return pl.BlockSpec((None,) + tuple(shape[1:]), lambda *_: (0,) * len(shape),
                        pipeline_mode=pl.Buffered(1))


def kernel(x, mem, norm_mix_g, w_in, conv_w, gm_ln_g, gm_ln_b, gm_ws, gm_bs, w_out,
           norm_x_g, norm_mem_g, w_q, w_kv, w_xo, norm_final_g):
    b, s, d = x.shape
    m_len = mem.shape[1]
    assert w_in.shape[0] == 1, "the final norm is fused into the (single) layer's attention call"
    assert s % TM == 0 and TM % CHUNK == 0 and d % CW == 0
    assert s % TM_ATTN == 0 and TM_ATTN % ATTN_HALF == 0
    bf16 = jnp.bfloat16
    f32 = jnp.float32
    params = lambda limit: pltpu.CompilerParams(
        dimension_semantics=("arbitrary", "arbitrary"), vmem_limit_bytes=limit)
    row = lambda a: a.reshape(1, -1)
    hbm = pl.BlockSpec(memory_space=pl.ANY)
    tile = pl.BlockSpec((None, TM, d), lambda i, t: (i, t, 0))
    sems = lambda n: pltpu.SemaphoreType.DMA((n,))

    x = pl.pallas_call(
        _mixer_kernel,
        grid=(b, s // TM),
        in_specs=[tile,
                  _resident((1, d)),
                  hbm,
                  hbm,
                  _resident((1, d)), _resident((1, d)),
                  _resident(gm_ws.shape[1:]),
                  _resident_layer(gm_bs.shape),
                  hbm],
        out_specs=tile,
        out_shape=jax.ShapeDtypeStruct((b, s, d), f32),
        scratch_shapes=[pltpu.VMEM((d, N_SECT * d), bf16),
                        pltpu.VMEM((2 * d, d), bf16),
                        _stage(N_SECT * d, MIX_STAGES), _stage(d, MIX_STAGES),
                        sems(MIX_STAGES),
                        pltpu.VMEM((TM, d), bf16),
                        pltpu.VMEM((2, TM, N_SECT * CW), f32),
                        pltpu.VMEM((SUBLANES, d), f32),
                        pltpu.VMEM((TM, 2 * d), bf16),
                        pltpu.VMEM((CHUNK, gm_bs.shape[1]), f32),
                        pltpu.VMEM((PARAM_ROWS, 1, d), f32),
                        pltpu.SemaphoreType.DMA(())],
        compiler_params=params(MIX_VMEM_LIMIT),
        name="mixer",
    )(x, row(norm_mix_g[0]), w_in[0], conv_w.reshape(CONV_K, 1, d), row(gm_ln_g[0]), row(gm_ln_b[0]),
      gm_ws[0], gm_bs, w_out[0])

    attn_tile = pl.BlockSpec((None, TM_ATTN, d), lambda i, t: (i, t, 0))
    return pl.pallas_call(
        _attn_kernel,
        grid=(b, s // TM_ATTN),
        in_specs=[attn_tile,
                  pl.BlockSpec((None, m_len, d), lambda i, t: (i, 0, 0)),
                  _resident((1, d)), _resident((1, d)),
                  hbm, hbm, hbm,
                  _resident((1, d))],
        out_specs=attn_tile,
        out_shape=jax.ShapeDtypeStruct((b, s, d), f32),
        scratch_shapes=[pltpu.VMEM((d, d), bf16),
                        pltpu.VMEM((d, 2 * d), bf16),
                        pltpu.VMEM((d, d), bf16),
                        _stage(d, ATTN_STAGES), _stage(2 * d, ATTN_STAGES),
                        sems(ATTN_STAGES),
                        pltpu.VMEM((d, X_HEADS * m_len), bf16),
                        pltpu.VMEM((X_HEADS * m_len, d), bf16),
                        pltpu.VMEM((TM_ATTN, X_HEADS * m_len), bf16)],
        compiler_params=params(ATTN_VMEM_LIMIT),
        name="xattn",
    )(x, mem, row(norm_x_g[0]), row(norm_mem_g[0]), w_q[0], w_kv[0], w_xo[0],
      row(norm_final_g))
```
